```python
import jax, jax.numpy as jnp
from jax import lax

D_MODEL = 1024
BATCH = 2
SEQ = 8192
DEPTH = 2

D_MIX = D_MODEL
D_RWKV = D_MIX // 2
D_CONV = D_MIX - D_RWKV
HEAD_DIM = 64
N_RWKV_HEADS = D_RWKV // HEAD_DIM
N_CONV_GROUPS = D_CONV // HEAD_DIM
W_LORA = 32
A_LORA = 32
V_LORA = 32
G_LORA = 96
CONV_WIDTH = 31
FFN_CONV_WIDTH = 3
D_FF = 2816
GLU_COLS = 2 * D_CONV
RWKV_COLS = 3 * D_RWKV + W_LORA + A_LORA + G_LORA
COLS_FIRST = GLU_COLS + RWKV_COLS
COLS_REST = COLS_FIRST + V_LORA
RMS_EPS = 1e-6
LN_EPS = 1e-5
GN_EPS = 64e-5

kernel_name = "rwkv7_conformer_hymba_convffn_trunk"


def _rms_norm(x, g):
    xf = x.astype(jnp.float32)
    y = xf * lax.rsqrt(jnp.mean(xf * xf, axis=-1, keepdims=True) + RMS_EPS)
    return (y * g.astype(jnp.float32)).astype(x.dtype)


def _layer_norm(x, g, b):
    xf = x.astype(jnp.float32)
    mu = jnp.mean(xf, axis=-1, keepdims=True)
    var = jnp.mean(jnp.square(xf - mu), axis=-1, keepdims=True)
    y = (xf - mu) * lax.rsqrt(var + LN_EPS)
    return (y * g.astype(jnp.float32) + b.astype(jnp.float32)).astype(x.dtype)


def _token_shift(p):
    return jnp.pad(p, ((0, 0), (1, 0), (0, 0)))[:, :-1]


def _causal_dwconv(x, w, b):
    K = w.shape[0]
    y = lax.conv_general_dilated(
        x, w[:, None, :].astype(x.dtype), window_strides=(1,), padding=((K - 1, 0),),
        dimension_numbers=("NWC", "WIO", "NWC"), feature_group_count=x.shape[-1])
    return y + b.astype(x.dtype)


def _wkv7(r, decay, k, v, a_vec, b_vec):
    B, T, H, N = r.shape

    def step(S, inp):
        r_t, w_t, k_t, v_t, a_t, b_t = inp
        sa = jnp.einsum("bhvk,bhk->bhv", S, a_t)
        S = S * w_t[:, :, None, :] + sa[..., None] * b_t[:, :, None, :] + v_t[..., None] * k_t[:, :, None, :]
        y = jnp.einsum("bhvk,bhk->bhv", S, r_t)
        return S, y

    xs = tuple(jnp.moveaxis(t.astype(jnp.float32), 1, 0) for t in (r, decay, k, v, a_vec, b_vec))
    S0 = jnp.zeros((B, H, N, N), jnp.float32)
    _, ys = lax.scan(step, S0, xs)
    return jnp.moveaxis(ys, 0, 1)


def _head_group_norm(y, g, b):
    B, T, H, N = y.shape
    mu = jnp.mean(y, axis=-1, keepdims=True)
    var = jnp.mean(jnp.square(y - mu), axis=-1, keepdims=True)
    y = ((y - mu) * lax.rsqrt(var + GN_EPS)).reshape(B, T, H * N)
    return y * g.astype(jnp.float32) + b.astype(jnp.float32)


def _hybrid_mixer(h, w_in, mu, vres_v0, vres_up, v_first, decay_w0, decay_up, iclr_a0, iclr_up,
                  gate_up, k_k, k_a, r_k, lnx_g, lnx_b, cconv_w, cconv_b, cln_g, cln_b, w_out):
    B, T, _ = h.shape
    H, N = N_RWKV_HEADS, HEAD_DIM
    p = h @ w_in.astype(h.dtype)
    u = p[..., :GLU_COLS]
    q = p[..., GLU_COLS:]
    q = q + (_token_shift(q) - q) * mu.astype(h.dtype)
    o1, o2, o3 = D_RWKV, 2 * D_RWKV, 3 * D_RWKV
    o4, o5, o6 = o3 + W_LORA, o3 + W_LORA + A_LORA, RWKV_COLS
    r, k, v = q[..., :o1], q[..., o1:o2], q[..., o2:o3]
    wd, ad, gd = q[..., o3:o4], q[..., o4:o5], q[..., o5:o6]

    wf = (decay_w0 + jnp.tanh(wd) @ decay_up).astype(jnp.float32)
    decay = jnp.exp(-jnp.exp(-jax.nn.softplus(-wf) - 0.5))
    if v_first is None:
        v_first = v
    else:
        vd = q[..., o6:]
        v = v + (v_first - v) * jax.nn.sigmoid(vres_v0 + vd @ vres_up)
    a = jax.nn.sigmoid(iclr_a0 + ad @ iclr_up)
    g = jax.nn.sigmoid(gd) @ gate_up
    kk = (k * k_k).reshape(B, T, H, N).astype(jnp.float32)
    kk = kk * lax.rsqrt(jnp.maximum(jnp.sum(kk * kk, axis=-1, keepdims=True), 1e-24))
    k = k * (1 + (a - 1) * k_a)
    r_h = r.reshape(B, T, H, N)
    k_h = k.reshape(B, T, H, N)
    v_h = v.reshape(B, T, H, N)
    a_h = a.reshape(B, T, H, N).astype(jnp.float32)
    y = _wkv7(r_h, decay.reshape(B, T, H, N), k_h, v_h, -kk, kk * a_h)
    y = _head_group_norm(y, lnx_g, lnx_b).astype(h.dtype)
    bonus = jnp.sum(r_h * k_h * r_k.astype(h.dtype), axis=-1, keepdims=True) * v_h
    rwkv_out = (y + bonus.reshape(B, T, D_RWKV)) * g

    c = u[..., :D_CONV] * jax.nn.sigmoid(u[..., D_CONV:])
    c = _causal_dwconv(c, cconv_w, cconv_b)
    c = jax.nn.silu(_layer_norm(c, cln_g, cln_b))

    mix = jnp.concatenate([rwkv_out, c], axis=-1) @ w_out.astype(h.dtype)
    return mix, v_first


def _conv_ffn(h, ffn_up, ffn_conv_w, ffn_conv_b, ffn_down):
    hid = h @ ffn_up.astype(h.dtype)
    hid = _causal_dwconv(hid, ffn_conv_w, ffn_conv_b)
    gate, up = hid[..., :D_FF], hid[..., D_FF:]
    return (jax.nn.gelu(gate, approximate=True) * up) @ ffn_down.astype(h.dtype)


def setup_inputs(seed: int = 0) -> dict:
    key = jax.random.key(seed)
    ks = iter(jax.random.split(key, 40))
    f32 = jnp.float32

    def nrm(shape, scale):
        return jax.random.normal(next(ks), shape, f32) * scale

    def unif(shape, lo, hi):
        return jax.random.uniform(next(ks), shape, f32, lo, hi)

    L, Lr = DEPTH, DEPTH - 1
    return {
        "x": nrm((BATCH, SEQ, D_MODEL), 1.0),
        "w_in_first": nrm((D_MODEL, COLS_FIRST), D_MODEL ** -0.5),
        "mu_first": unif((RWKV_COLS,), 0.0, 1.0),
        "w_in_rest": nrm((Lr, D_MODEL, COLS_REST), D_MODEL ** -0.5),
        "mu_rest": unif((Lr, RWKV_COLS + V_LORA), 0.0, 1.0),
        "vres_v0": nrm((Lr, D_RWKV), 0.5),
        "vres_up": nrm((Lr, V_LORA, D_RWKV), 0.1),
        "decay_w0": unif((L, D_RWKV), -6.0, 1.0),
        "decay_up": nrm((L, W_LORA, D_RWKV), 0.1),
        "iclr_a0": nrm((L, D_RWKV), 0.5),
        "iclr_up": nrm((L, A_LORA, D_RWKV), 0.1),
        "gate_up": nrm((L, G_LORA, D_RWKV), G_LORA ** -0.5),
        "k_k": 0.85 + nrm((L, D_RWKV), 0.05),
        "k_a": 1.0 + nrm((L, D_RWKV), 0.05),
        "r_k": nrm((L, N_RWKV_HEADS, HEAD_DIM), 0.1),
        "lnx_g": 1.0 + nrm((L, D_RWKV), 0.05),
        "lnx_b": nrm((L, D_RWKV), 0.01),
        "cconv_w": nrm((L, CONV_WIDTH, D_CONV), CONV_WIDTH ** -0.5),
        "cconv_b": nrm((L, D_CONV), 0.01),
        "cln_g": 1.0 + nrm((L, D_CONV), 0.05),
        "cln_b": nrm((L, D_CONV), 0.01),
        "w_out": nrm((L, D_MIX, D_MODEL), D_MIX ** -0.5),
        "norm_pre_mix": 1.0 + nrm((L, D_MODEL), 0.05),
        "norm_post_mix": 1.0 + nrm((L, D_MODEL), 0.05),
        "norm_pre_ffn": 1.0 + nrm((L, D_MODEL), 0.05),
        "norm_post_ffn": 1.0 + nrm((L, D_MODEL), 0.05),
        "ffn_up": nrm((L, D_MODEL, 2 * D_FF), D_MODEL ** -0.5),
        "ffn_conv_w": nrm((L, FFN_CONV_WIDTH, 2 * D_FF), FFN_CONV_WIDTH ** -0.5),
        "ffn_conv_b": nrm((L, 2 * D_FF), 0.01),
        "ffn_down": nrm((L, D_FF, D_MODEL), D_FF ** -0.5),
    }


def reference(x, w_in_first, mu_first, w_in_rest, mu_rest, vres_v0, vres_up, decay_w0, decay_up,
              iclr_a0, iclr_up, gate_up, k_k, k_a, r_k, lnx_g, lnx_b, cconv_w, cconv_b, cln_g, cln_b,
              w_out, norm_pre_mix, norm_post_mix, norm_pre_ffn, norm_post_ffn, ffn_up, ffn_conv_w,
              ffn_conv_b, ffn_down):
    v_first = None
    for i in range(DEPTH):
        if i == 0:
            w_in_i, mu_i, v0_i, vup_i = w_in_first, mu_first, None, None
        else:
            w_in_i, mu_i, v0_i, vup_i = w_in_rest[i - 1], mu_rest[i - 1], vres_v0[i - 1], vres_up[i - 1]
        h = _rms_norm(x, norm_pre_mix[i])
        mix, v_first = _hybrid_mixer(
            h, w_in_i, mu_i, v0_i, vup_i, v_first, decay_w0[i], decay_up[i], iclr_a0[i], iclr_up[i],
            gate_up[i], k_k[i], k_a[i], r_k[i], lnx_g[i], lnx_b[i], cconv_w[i], cconv_b[i],
            cln_g[i], cln_b[i], w_out[i])
        x = x + _rms_norm(mix, norm_post_mix[i])
        h = _rms_norm(x, norm_pre_ffn[i])
        f = _conv_ffn(h, ffn_up[i], ffn_conv_w[i], ffn_conv_b[i], ffn_down[i])
        x = x + _rms_norm(f, norm_post_ffn[i])
    return x
```

```python
import functools
import math

import jax
import jax.numpy as jnp
import numpy as np
from jax.experimental import pallas as pl
from jax.experimental.pallas import tpu as pltpu

F32 = jnp.float32
BF16 = jnp.bfloat16

D_MODEL = 1024
D_RWKV = 512
D_CONV = 512
HEAD_DIM = 64
W_LORA, A_LORA, V_LORA, G_LORA = 32, 32, 32, 96
CONV_WIDTH = 31
FFN_CONV_WIDTH = 3
D_FF = 2816
GLU_COLS = 2 * D_CONV
RWKV_COLS = 3 * D_RWKV + W_LORA + A_LORA + G_LORA
RMS_EPS = 1e-6
LN_EPS = 1e-5
GN_EPS = 64e-5

LANES = 128
SUBLANES = 8
MXU_DIM = 256
LORA_OFF = 3 * D_RWKV
LORA_BLK = 2 * LANES
Q_COLS = LORA_OFF + LORA_BLK
P_COLS = GLU_COLS + Q_COLS
CHUNK = 64
GROUP = MXU_DIM
HEADS_PER_GROUP = GROUP // HEAD_DIM
N_GROUPS = D_RWKV // GROUP
CONV_HALO = 32
FF_BLK = 256
VMEM_LIMIT = 56 * 1024 * 1024

TT_IN = 256
TT_WKV = 256
TT_MIX = 256
TT_FFN = 256

_HI = jax.lax.Precision.HIGHEST


def _dot(a, b, precision=None):
    return jax.lax.dot_general(a, b, (((1,), (0,)), ((), ())), precision=precision,
                               preferred_element_type=F32)


def _dot_nt(a, b, precision=None):
    return jax.lax.dot_general(a, b, (((1,), (1,)), ((), ())), precision=precision,
                               preferred_element_type=F32)


def _dot_tn(a, b, precision=None):
    return jax.lax.dot_general(a, b, (((0,), (0,)), ((), ())), precision=precision,
                               preferred_element_type=F32)


def _bdot(a, w_bf16):
    return _dot(a.astype(BF16), w_bf16)


def _seg_sum(x, seg_ones):
    hi = x.astype(BF16)
    lo = (x - hi.astype(F32)).astype(BF16)
    return _dot(hi, seg_ones) + _dot(lo, seg_ones)


def _rms(x, g):
    return x * jax.lax.rsqrt(jnp.mean(x * x, axis=-1, keepdims=True) + RMS_EPS) * g


def _shift_rows(x, carry, n):
    tt = x.shape[0]
    row = jax.lax.broadcasted_iota(jnp.int32, x.shape, 0)
    out = pltpu.roll(x, n, 0)
    for i in range(n):
        out = jnp.where(row == i, carry[SUBLANES - n + i:SUBLANES - n + i + 1, :], out)
    del tt
    return out


def _inproj_kernel(has_vres, x_ref, gpre_ref, win_ref, mu_ref, w0_ref, wd_ref, a0_ref, wa_ref,
                   wg_ref, kk_ref, ka_ref, seg_ref, *rest):
    if has_vres:
        (v0_ref, wv_ref, vfirst_ref, r_out, lw_out, k_out, v_out, kk_out, b_out, g_out, c_out,
         carry_ref) = rest
    else:
        (r_out, lw_out, k_out, v_out, kk_out, b_out, g_out, c_out, carry_ref) = rest
    t = pl.program_id(1)

    @pl.when(t == 0)
    def _():
        carry_ref[...] = jnp.zeros_like(carry_ref)

    x = x_ref[0]
    h = _rms(x, gpre_ref[...])
    p = _bdot(h, win_ref[...])
    c_out[0] = p[:, :D_CONV] * jax.nn.sigmoid(p[:, D_CONV:GLU_COLS])
    q = p[:, GLU_COLS:]
    prev = _shift_rows(q, carry_ref[...], 1)
    tt = q.shape[0]
    carry_ref[...] = q[tt - SUBLANES:, :]
    q = q + (prev - q) * mu_ref[...]
    r = q[:, :D_RWKV]
    k = q[:, D_RWKV:2 * D_RWKV]
    v = q[:, 2 * D_RWKV:3 * D_RWKV]
    blk = q[:, LORA_OFF:]
    wf = w0_ref[...] + _bdot(jnp.tanh(blk), wd_ref[...])
    lw_out[0] = -math.exp(-0.5) * jax.nn.sigmoid(wf)
    a = jax.nn.sigmoid(a0_ref[...] + _bdot(blk, wa_ref[...]))
    g_out[0] = _bdot(jax.nn.sigmoid(blk), wg_ref[...])
    if has_vres:
        v = v + (vfirst_ref[0] - v) * jax.nn.sigmoid(v0_ref[...] + _bdot(blk, wv_ref[...]))
    kk = k * kk_ref[...]
    ss = _seg_sum(kk * kk, seg_ref[...])
    kk = kk * jax.lax.rsqrt(jnp.maximum(ss, 1e-24))
    r_out[0] = r
    k_out[0] = k * (1.0 + (a - 1.0) * ka_ref[...])
    v_out[0] = v
    kk_out[0] = kk
    b_out[0] = kk * a


def _full(shape):
    return pl.BlockSpec(shape, lambda b, t: (0,) * len(shape))


def _tile(tt, width):
    return pl.BlockSpec((1, tt, width), lambda b, t: (b, t, 0))


def _params():
    return pltpu.CompilerParams(dimension_semantics=("arbitrary", "arbitrary"),
                                vmem_limit_bytes=VMEM_LIMIT)


def _inproj(x, gpre, win, mu, w0, wd, a0, wa, wg, k_k, k_a, seg, vres):
    B, T, _ = x.shape
    tt = TT_IN
    has_vres = vres is not None
    row = lambda w: _full((1, w))
    in_specs = [_tile(tt, D_MODEL), row(D_MODEL), _full((D_MODEL, P_COLS)), row(Q_COLS),
                row(D_RWKV), _full((LORA_BLK, D_RWKV)), row(D_RWKV), _full((LORA_BLK, D_RWKV)),
                _full((LORA_BLK, D_RWKV)), row(D_RWKV), row(D_RWKV), _full((D_RWKV, D_RWKV))]
    args = [x, gpre, win, mu, w0, wd, a0, wa, wg, k_k, k_a, seg]
    if has_vres:
        v0, wv, vfirst = vres
        in_specs += [row(D_RWKV), _full((LORA_BLK, D_RWKV)), _tile(tt, D_RWKV)]
        args += [v0, wv, vfirst]
    out = jax.ShapeDtypeStruct((B, T, D_RWKV), F32)
    return pl.pallas_call(
        functools.partial(_inproj_kernel, has_vres),
        grid=(B, T // tt),
        in_specs=in_specs,
        out_specs=[_tile(tt, D_RWKV)] * 8,
        out_shape=[out] * 8,
        scratch_shapes=[pltpu.VMEM((SUBLANES, Q_COLS), F32)],
        compiler_params=_params(),
        name="inproj",
    )(*args)


def _wkv_kernel(r_ref, lw_ref, k_ref, v_ref, kk_ref, b_ref, y_ref, z_ref):
    t = pl.program_id(1)

    @pl.when(t == 0)
    def _():
        z_ref[...] = jnp.zeros_like(z_ref)

    n = GROUP
    ri = jax.lax.broadcasted_iota(jnp.int32, (n, n), 0)
    ci = jax.lax.broadcasted_iota(jnp.int32, (n, n), 1)
    shift = int(math.log2(CHUNK))
    same = (ri >> shift) == (ci >> shift)
    rin, cin = ri & (CHUNK - 1), ci & (CHUNK - 1)
    strict = same & (rin > cin)
    incl = same & (rin >= cin)
    eye = ri == ci
    ti = jax.lax.broadcasted_iota(jnp.int32, (CHUNK, CHUNK), 0)
    tj = jax.lax.broadcasted_iota(jnp.int32, (CHUNK, CHUNK), 1)
    tri = jnp.where(ti >= tj, 1.0, 0.0)

    def bd(xl):
        return jnp.where(same, jnp.concatenate([xl] * HEADS_PER_GROUP, axis=0), 0.0)

    def fold(xb):
        out = xb[0:CHUNK]
        for hh in range(1, HEADS_PER_GROUP):
            out = out + xb[hh * CHUNK:(hh + 1) * CHUNK]
        return out

    n_chunks = r_ref.shape[1] // CHUNK

    def chunk_body(c, carry):
        rows = pl.ds(pl.multiple_of(c * CHUNK, CHUNK), CHUNK)
        for gi in range(N_GROUPS):
            lanes = pl.ds(gi * GROUP, GROUP)
            lw = lw_ref[0, rows, lanes]
            G = _dot(tri, lw, _HI)
            glast = G[CHUNK - 1:CHUNK, :]
            eG, eGp, enG, eGC = jnp.exp(G), jnp.exp(G - lw), jnp.exp(-G), jnp.exp(glast - G)
            r, k, v = r_ref[0, rows, lanes], k_ref[0, rows, lanes], v_ref[0, rows, lanes]
            a, b = -kk_ref[0, rows, lanes], b_ref[0, rows, lanes]
            rt = r * eG
            am, rm, bm, km, vm = bd(a * eGp), bd(rt), bd(b * enG), bd(k * enG), bd(v)
            a_ab = jnp.where(strict, _dot_nt(am, bm, _HI), 0.0)
            a_ak = jnp.where(strict, _dot_nt(am, km, _HI), 0.0)
            a_rb = jnp.where(incl, _dot_nt(rm, bm, _HI), 0.0)
            a_rk = jnp.where(incl, _dot_nt(rm, km, _HI), 0.0)
            tinv = jnp.where(eye, 1.0, 0.0) + a_ab
            ak = a_ab
            for _ in range(int(math.log2(CHUNK)) - 1):
                ak = _dot(ak, ak, _HI)
                tinv = tinv + _dot(tinv, ak, _HI)
            u0 = _dot(tinv, _dot(a_ak, vm, _HI), _HI)
            wt = _dot(tinv, am, _HI)
            y0 = fold(_dot(a_rk, vm, _HI) + _dot(a_rb, u0, _HI))
            rbar = rt + fold(_dot(a_rb, wt, _HI))
            bhm, khm = bd(b * eGC), bd(k * eGC)
            m = jnp.where(eye, jnp.exp(glast), 0.0) + _dot_tn(bhm, wt, _HI)
            qn = _dot_tn(bhm, u0, _HI) + _dot_tn(khm, vm, _HI)
            z0 = z_ref[gi]
            y_ref[0, rows, lanes] = y0 + _dot(rbar, z0, _HI)
            z_ref[gi] = _dot(m, z0, _HI) + qn
        return carry

    jax.lax.fori_loop(0, n_chunks, chunk_body, 0)


def _wkv(r, lw, k, v, kk, bvec):
    B, T, _ = r.shape
    tt = TT_WKV
    return pl.pallas_call(
        _wkv_kernel,
        grid=(B, T // tt),
        in_specs=[_tile(tt, D_RWKV)] * 6,
        out_specs=_tile(tt, D_RWKV),
        out_shape=jax.ShapeDtypeStruct((B, T, D_RWKV), F32),
        scratch_shapes=[pltpu.VMEM((N_GROUPS, GROUP, GROUP), F32)],
        compiler_params=_params(),
        name="wkv",
    )(r, lw, k, v, kk, bvec)


def _mix_kernel(y_ref, r_ref, k_ref, v_ref, g_ref, c_ref, x_ref, lng_ref, lnb_ref, rk_ref,
                cw_ref, cb_ref, clg_ref, clb_ref, wo1_ref, wo2_ref, gpost_ref, seg_ref,
                o_ref, ext_ref):
    t = pl.program_id(1)
    tt = c_ref.shape[1]

    @pl.when(t == 0)
    def _():
        ext_ref[0:CONV_HALO, :] = jnp.zeros((CONV_HALO, D_CONV), F32)

    seg = seg_ref[...]
    inv_n = 1.0 / HEAD_DIM
    y = y_ref[0]
    mean = _seg_sum(y, seg) * inv_n
    yc = y - mean
    var = _seg_sum(yc * yc, seg) * inv_n
    gn = yc * jax.lax.rsqrt(var + GN_EPS) * lng_ref[...] + lnb_ref[...]
    v = v_ref[0]
    bonus = _seg_sum(r_ref[0] * k_ref[0] * rk_ref[...], seg) * v
    rw = (gn + bonus) * g_ref[0]

    ext_ref[CONV_HALO:CONV_HALO + tt, :] = c_ref[0]
    off = CONV_HALO - (CONV_WIDTH - 1)
    conv = ext_ref[pl.ds(off, tt), :] * cw_ref[0:1, :]
    for j in range(1, CONV_WIDTH):
        conv = conv + ext_ref[pl.ds(off + j, tt), :] * cw_ref[j:j + 1, :]
    conv = conv + cb_ref[...]
    ext_ref[0:CONV_HALO, :] = ext_ref[tt:tt + CONV_HALO, :]
    mu = jnp.mean(conv, axis=-1, keepdims=True)
    cc = conv - mu
    cvar = jnp.mean(cc * cc, axis=-1, keepdims=True)
    cn = cc * jax.lax.rsqrt(cvar + LN_EPS) * clg_ref[...] + clb_ref[...]
    cs = cn * jax.nn.sigmoid(cn)

    mix = _bdot(rw, wo1_ref[...]) + _bdot(cs, wo2_ref[...])
    o_ref[0] = x_ref[0] + _rms(mix, gpost_ref[...])


def _mix(y, r, k, v, g, c, x, lng, lnb, rk, cw, cb, clg, clb, wo1, wo2, gpost, seg):
    B, T, _ = x.shape
    tt = TT_MIX
    row = lambda w: _full((1, w))
    return pl.pallas_call(
        _mix_kernel,
        grid=(B, T // tt),
        in_specs=[_tile(tt, D_RWKV)] * 6 + [_tile(tt, D_MODEL), row(D_RWKV), row(D_RWKV),
                                            row(D_RWKV), _full((CONV_HALO, D_CONV)), row(D_CONV),
                                            row(D_CONV), row(D_CONV), _full((D_RWKV, D_MODEL)),
                                            _full((D_CONV, D_MODEL)), row(D_MODEL),
                                            _full((D_RWKV, D_RWKV))],
        out_specs=_tile(tt, D_MODEL),
        out_shape=jax.ShapeDtypeStruct((B, T, D_MODEL), F32),
        scratch_shapes=[pltpu.VMEM((CONV_HALO + tt, D_CONV), F32)],
        compiler_params=_params(),
        name="mixout",
    )(y, r, k, v, g, c, x, lng, lnb, rk, cw, cb, clg, clb, wo1, wo2, gpost, seg)


def _ffn_kernel(x_ref, gpre_ref, up_ref, cw_ref, cb_ref, down_ref, gpost_ref, o_ref, carry_ref):
    t = pl.program_id(1)

    @pl.when(t == 0)
    def _():
        carry_ref[...] = jnp.zeros_like(carry_ref)

    x = x_ref[0]
    tt = x.shape[0]
    h = _rms(x, gpre_ref[...]).astype(BF16)
    acc = jnp.zeros((tt, D_MODEL), F32)
    for cblk in range(D_FF // FF_BLK):
        parts = []
        for base in (0, D_FF):
            cols = pl.ds(base + cblk * FF_BLK, FF_BLK)
            hid = _dot(h, up_ref[:, cols])
            carry = carry_ref[:, cols]
            h1 = _shift_rows(hid, carry, 1)
            h2 = _shift_rows(hid, carry, 2)
            carry_ref[:, cols] = hid[tt - SUBLANES:, :]
            parts.append(h2 * cw_ref[0:1, cols] + h1 * cw_ref[1:2, cols] + hid * cw_ref[2:3, cols]
                         + cb_ref[:, cols])
        gate, upv = parts
        act = jax.nn.gelu(gate, approximate=True) * upv
        acc = acc + _bdot(act, down_ref[pl.ds(cblk * FF_BLK, FF_BLK), :])
    o_ref[0] = x + _rms(acc, gpost_ref[...])


def _ffn(x, gpre, up, cw, cb, down, gpost):
    B, T, _ = x.shape
    tt = TT_FFN
    row = lambda w: _full((1, w))
    return pl.pallas_call(
        _ffn_kernel,
        grid=(B, T // tt),
        in_specs=[_tile(tt, D_MODEL), row(D_MODEL), _full((D_MODEL, 2 * D_FF)),
                  _full((SUBLANES, 2 * D_FF)), row(2 * D_FF), _full((D_FF, D_MODEL)),
                  row(D_MODEL)],
        out_specs=_tile(tt, D_MODEL),
        out_shape=jax.ShapeDtypeStruct((B, T, D_MODEL), F32),
        scratch_shapes=[pltpu.VMEM((SUBLANES, 2 * D_FF), F32)],
        compiler_params=_params(),
        name="convffn",
    )(x, gpre, up, cw, cb, down, gpost)


def _pad_rows(w, off, rows):
    return jnp.zeros((rows, w.shape[1]), w.dtype).at[off:off + w.shape[0]].set(w)


def _seg_ones():
    idx = np.arange(D_RWKV) // HEAD_DIM
    return jnp.asarray((idx[:, None] == idx[None, :]).astype(np.float32), dtype=BF16)


def kernel(x, w_in_first, mu_first, w_in_rest, mu_rest, vres_v0, vres_up, decay_w0, decay_up, iclr_a0, iclr_up, gate_up, k_k, k_a, r_k, lnx_g, lnx_b, cconv_w, cconv_b, cln_g, cln_b, w_out, norm_pre_mix, norm_post_mix, norm_pre_ffn, norm_post_ffn, ffn_up, ffn_conv_w, ffn_conv_b, ffn_down):
    depth = decay_w0.shape[0]
    seg = _seg_ones()
    row = lambda a: a.reshape(1, -1)
    v_first = None
    for i in range(depth):
        if i == 0:
            w_in_i, mu_i = w_in_first, mu_first
        else:
            w_in_i, mu_i = w_in_rest[i - 1], mu_rest[i - 1]
        win = jnp.pad(w_in_i, ((0, 0), (0, P_COLS - w_in_i.shape[1]))).astype(BF16)
        mu = jnp.pad(mu_i, (0, Q_COLS - mu_i.shape[0])).reshape(1, Q_COLS)
        wd = _pad_rows(decay_up[i], 0, LORA_BLK).astype(BF16)
        wa = _pad_rows(iclr_up[i], W_LORA, LORA_BLK).astype(BF16)
        wg = _pad_rows(gate_up[i], W_LORA + A_LORA, LORA_BLK).astype(BF16)
        vres = None
        if i > 0:
            wv = _pad_rows(vres_up[i - 1], W_LORA + A_LORA + G_LORA, LORA_BLK).astype(BF16)
            vres = (row(vres_v0[i - 1]), wv, v_first)
        r, lw, k, v, kk, bvec, g, c = _inproj(
            x, row(norm_pre_mix[i]), win, mu, row(decay_w0[i]), wd, row(iclr_a0[i]), wa, wg,
            row(k_k[i]), row(k_a[i]), seg, vres)
        if i == 0:
            v_first = v
        y = _wkv(r, lw, k, v, kk, bvec)
        cw = jnp.pad(cconv_w[i], ((0, CONV_HALO - CONV_WIDTH), (0, 0)))
        wo = w_out[i].astype(BF16)
        x = _mix(y, r, k, v, g, c, x, row(lnx_g[i]), row(lnx_b[i]), row(r_k[i]), cw,
                 row(cconv_b[i]), row(cln_g[i]), row(cln_b[i]), wo[:D_RWKV], wo[D_RWKV:],
                 row(norm_post_mix[i]), seg)
        fcw = jnp.pad(ffn_conv_w[i], ((0, SUBLANES - FFN_CONV_WIDTH), (0, 0)))
        x = _ffn(x, row(norm_pre_ffn[i]), ffn_up[i].astype(BF16), fcw, row(ffn_conv_b[i]),
                 ffn_down[i].astype(BF16), row(norm_post_ffn[i]))
    return x
```

```python
import functools
import math

import jax
import jax.numpy as jnp
import numpy as np
from jax.experimental import pallas as pl
from jax.experimental.pallas import tpu as pltpu

F32 = jnp.float32
BF16 = jnp.bfloat16

D_MODEL = 1024
D_RWKV = 512
D_CONV = 512
HEAD_DIM = 64
W_LORA, A_LORA, V_LORA, G_LORA = 32, 32, 32, 96
CONV_WIDTH = 31
FFN_CONV_WIDTH = 3
D_FF = 2816
GLU_COLS = 2 * D_CONV
RWKV_COLS = 3 * D_RWKV + W_LORA + A_LORA + G_LORA
RMS_EPS = 1e-6
LN_EPS = 1e-5
GN_EPS = 64e-5

LANES = 128
SUBLANES = 8
MXU_DIM = 256
LORA_OFF = 3 * D_RWKV
LORA_BLK = 2 * LANES
Q_COLS = LORA_OFF + LORA_BLK
P_COLS = GLU_COLS + Q_COLS
CHUNK = 64
GROUP = MXU_DIM
HEADS_PER_GROUP = GROUP // HEAD_DIM
N_GROUPS = D_RWKV // GROUP
CONV_HALO = 32
FF_BLK = 256
VMEM_LIMIT = 56 * 1024 * 1024

TT_IN = 256
TT_WKV = 256
TT_MIX = 256
TT_FFN = 256

_HI = jax.lax.Precision.HIGHEST


def _dot(a, b, precision=None):
    return jax.lax.dot_general(a, b, (((1,), (0,)), ((), ())), precision=precision,
                               preferred_element_type=F32)


def _dot_nt(a, b, precision=None):
    return jax.lax.dot_general(a, b, (((1,), (1,)), ((), ())), precision=precision,
                               preferred_element_type=F32)


def _dot_tn(a, b, precision=None):
    return jax.lax.dot_general(a, b, (((0,), (0,)), ((), ())), precision=precision,
                               preferred_element_type=F32)


def _bdot(a, w_bf16):
    return _dot(a.astype(BF16), w_bf16)


def _seg_sum(x, seg_ones):
    hi = x.astype(BF16)
    lo = (x - hi.astype(F32)).astype(BF16)
    return _dot(hi, seg_ones) + _dot(lo, seg_ones)


def _rms(x, g):
    return x * jax.lax.rsqrt(jnp.mean(x * x, axis=-1, keepdims=True) + RMS_EPS) * g


def _shift_rows(x, carry, n):
    tt = x.shape[0]
    row = jax.lax.broadcasted_iota(jnp.int32, x.shape, 0)
    out = pltpu.roll(x, n, 0)
    for i in range(n):
        out = jnp.where(row == i, carry[SUBLANES - n + i:SUBLANES - n + i + 1, :], out)
    del tt
    return out


def _inproj_kernel(has_vres, x_ref, gpre_ref, win_ref, mu_ref, w0_ref, wd_ref, a0_ref, wa_ref,
                   wg_ref, kk_ref, ka_ref, seg_ref, *rest):
    if has_vres:
        (v0_ref, wv_ref, vfirst_ref, r_out, lw_out, k_out, v_out, kk_out, b_out, g_out, c_out,
         carry_ref) = rest
    else:
        (r_out, lw_out, k_out, v_out, kk_out, b_out, g_out, c_out, carry_ref) = rest
    t = pl.program_id(1)

    @pl.when(t == 0)
    def _():
        carry_ref[...] = jnp.zeros_like(carry_ref)

    x = x_ref[0]
    h = _rms(x, gpre_ref[...])
    p = _bdot(h, win_ref[...])
    c_out[0] = p[:, :D_CONV] * jax.nn.sigmoid(p[:, D_CONV:GLU_COLS])
    q = p[:, GLU_COLS:]
    prev = _shift_rows(q, carry_ref[...], 1)
    tt = q.shape[0]
    carry_ref[...] = q[tt - SUBLANES:, :]
    q = q + (prev - q) * mu_ref[...]
    r = q[:, :D_RWKV]
    k = q[:, D_RWKV:2 * D_RWKV]
    v = q[:, 2 * D_RWKV:3 * D_RWKV]
    blk = q[:, LORA_OFF:]
    wf = w0_ref[...] + _bdot(jnp.tanh(blk), wd_ref[...])
    lw_out[0] = -math.exp(-0.5) * jax.nn.sigmoid(wf)
    a = jax.nn.sigmoid(a0_ref[...] + _bdot(blk, wa_ref[...]))
    g_out[0] = _bdot(jax.nn.sigmoid(blk), wg_ref[...])
    if has_vres:
        v = v + (vfirst_ref[0] - v) * jax.nn.sigmoid(v0_ref[...] + _bdot(blk, wv_ref[...]))
    kk = k * kk_ref[...]
    ss = _seg_sum(kk * kk, seg_ref[...])
    kk = kk * jax.lax.rsqrt(jnp.maximum(ss, 1e-24))
    r_out[0] = r
    k_out[0] = k * (1.0 + (a - 1.0) * ka_ref[...])
    v_out[0] = v
    kk_out[0] = kk
    b_out[0] = kk * a


def _full(shape):
    return pl.BlockSpec(shape, lambda b, t: (0,) * len(shape))


def _tile(tt, width):
    return pl.BlockSpec((1, tt, width), lambda b, t: (b, t, 0))


def _params():
    return pltpu.CompilerParams(dimension_semantics=("arbitrary", "arbitrary"),
                                vmem_limit_bytes=VMEM_LIMIT)


def _inproj(x, gpre, win, mu, w0, wd, a0, wa, wg, k_k, k_a, seg, vres):
    B, T, _ = x.shape
    tt = TT_IN
    has_vres = vres is not None
    row = lambda w: _full((1, w))
    in_specs = [_tile(tt, D_MODEL), row(D_MODEL), _full((D_MODEL, P_COLS)), row(Q_COLS),
                row(D_RWKV), _full((LORA_BLK, D_RWKV)), row(D_RWKV), _full((LORA_BLK, D_RWKV)),
                _full((LORA_BLK, D_RWKV)), row(D_RWKV), row(D_RWKV), _full((D_RWKV, D_RWKV))]
    args = [x, gpre, win, mu, w0, wd, a0, wa, wg, k_k, k_a, seg]
    if has_vres:
        v0, wv, vfirst = vres
        in_specs += [row(D_RWKV), _full((LORA_BLK, D_RWKV)), _tile(tt, D_RWKV)]
        args += [v0, wv, vfirst]
    out = jax.ShapeDtypeStruct((B, T, D_RWKV), F32)
    return pl.pallas_call(
        functools.partial(_inproj_kernel, has_vres),
        grid=(B, T // tt),
        in_specs=in_specs,
        out_specs=[_tile(tt, D_RWKV)] * 8,
        out_shape=[out] * 8,
        scratch_shapes=[pltpu.VMEM((SUBLANES, Q_COLS), F32)],
        compiler_params=_params(),
        name="inproj",
    )(*args)


def _wkv_masks():
    i = np.arange(GROUP)
    same = (i[:, None] // CHUNK) == (i[None, :] // HEAD_DIM)
    rin, cin = i[:, None] % CHUNK, i[None, :] % CHUNK
    m = np.stack([same, same & (rin > cin), same & (rin >= cin), i[:, None] == i[None, :]])
    t = np.arange(CHUNK)
    return (jnp.asarray(m.astype(np.float32), dtype=BF16),
            jnp.asarray((t[:, None] >= t[None, :]).astype(np.float32), dtype=BF16))


def _wkv_kernel(r_ref, lw_ref, k_ref, v_ref, kk_ref, b_ref, mask_ref, tri_ref, y_ref, z_ref):
    t = pl.program_id(1)

    @pl.when(t == 0)
    def _():
        z_ref[...] = jnp.zeros_like(z_ref)

    def bd(xl):
        return jnp.concatenate([xl.astype(BF16)] * HEADS_PER_GROUP, axis=0) * mask_ref[0]

    def fold(xb):
        out = xb[0:CHUNK]
        for hh in range(1, HEADS_PER_GROUP):
            out = out + xb[hh * CHUNK:(hh + 1) * CHUNK]
        return out

    def masked(x, which):
        return x.astype(BF16) * mask_ref[which]

    n_chunks = r_ref.shape[1] // CHUNK

    insts = [(c, gi) for c in range(n_chunks) for gi in range(N_GROUPS)]
    eye = mask_ref[3]
    st = []
    for c, gi in insts:
        rows, lanes = pl.ds(c * CHUNK, CHUNK), pl.ds(gi * GROUP, GROUP)
        lw = lw_ref[0, rows, lanes]
        tri = tri_ref[...]
        p1 = lw.astype(BF16)
        r1 = lw - p1.astype(F32)
        p2 = r1.astype(BF16)
        p3 = (r1 - p2.astype(F32)).astype(BF16)
        G = _dot(tri, p1) + _dot(tri, p2) + _dot(tri, p3)
        glast = G[CHUNK - 1:CHUNK, :]
        eG, eGp, enG, eGC = jnp.exp(G), jnp.exp(G - lw), jnp.exp(-G), jnp.exp(glast - G)
        r, k, v = r_ref[0, rows, lanes], k_ref[0, rows, lanes], v_ref[0, rows, lanes]
        a, b = -kk_ref[0, rows, lanes], b_ref[0, rows, lanes]
        rt = r * eG
        am, rm, bm, km, vm = bd(a * eGp), bd(rt), bd(b * enG), bd(k * enG), bd(v)
        a_ab = masked(_dot_nt(am, bm), 1)
        s = dict(rt=rt, am=am, vm=vm, a_ab=a_ab, ak=a_ab, tinv=a_ab + eye,
                 a_ak=masked(_dot_nt(am, km), 1), a_rb=masked(_dot_nt(rm, bm), 2),
                 a_rk=masked(_dot_nt(rm, km), 2),
                 bht=(b * eGC).T.astype(BF16), kht=(k * eGC).T.astype(BF16),
                 vb=v.astype(BF16), pc=jnp.exp(glast))
        st.append(s)
    for _ in range(int(math.log2(CHUNK)) - 1):
        for s in st:
            s["ak"] = _dot(s["ak"], s["ak"]).astype(BF16)
        for s in st:
            s["tinv"] = _dot(s["tinv"], s["ak"] + eye).astype(BF16)
    for s in st:
        s["av"] = _dot(s["a_ak"], s["vm"]).astype(BF16)
    for s in st:
        s["u0"] = _dot(s["tinv"], s["av"])
        s["wt"] = _dot(s["tinv"], s["am"])
    same = mask_ref[0].astype(F32)
    for s in st:
        u0, wt = s["u0"], s["wt"]
        s["y0"] = fold(_dot(s["a_rk"], s["vm"]) + _dot(s["a_rb"], u0.astype(BF16)))
        s["rbar"] = (s["rt"] + fold(_dot(s["a_rb"], wt.astype(BF16)))).astype(BF16)
        m = _dot(s["bht"], fold(wt).astype(BF16)) * same + eye.astype(F32) * s["pc"]
        s["m"] = m.astype(BF16)
        s["q"] = (_dot(s["bht"], fold(u0).astype(BF16)) + _dot(s["kht"], s["vb"])) * same
    z = [z_ref[gi] for gi in range(N_GROUPS)]
    for (c, gi), s in zip(insts, st):
        rows, lanes = pl.ds(c * CHUNK, CHUNK), pl.ds(gi * GROUP, GROUP)
        z0 = z[gi].astype(BF16)
        y_ref[0, rows, lanes] = s["y0"] + _dot(s["rbar"], z0)
        z[gi] = _dot(s["m"], z0) + s["q"]
    for gi in range(N_GROUPS):
        z_ref[gi] = z[gi]


def _wkv(r, lw, k, v, kk, bvec):
    B, T, _ = r.shape
    tt = TT_WKV
    masks, tri = _wkv_masks()
    return pl.pallas_call(
        _wkv_kernel,
        grid=(B, T // tt),
        in_specs=[_tile(tt, D_RWKV)] * 6 + [_full(masks.shape), _full(tri.shape)],
        out_specs=_tile(tt, D_RWKV),
        out_shape=jax.ShapeDtypeStruct((B, T, D_RWKV), F32),
        scratch_shapes=[pltpu.VMEM((N_GROUPS, GROUP, GROUP), F32)],
        compiler_params=_params(),
        name="wkv",
    )(r, lw, k, v, kk, bvec, masks, tri)


def _mix_kernel(y_ref, r_ref, k_ref, v_ref, g_ref, c_ref, x_ref, lng_ref, lnb_ref, rk_ref,
                cw_ref, cb_ref, clg_ref, clb_ref, wo1_ref, wo2_ref, gpost_ref, seg_ref,
                o_ref, ext_ref):
    t = pl.program_id(1)
    tt = c_ref.shape[1]

    @pl.when(t == 0)
    def _():
        ext_ref[0:CONV_HALO, :] = jnp.zeros((CONV_HALO, D_CONV), F32)

    seg = seg_ref[...]
    inv_n = 1.0 / HEAD_DIM
    y = y_ref[0]
    mean = _seg_sum(y, seg) * inv_n
    yc = y - mean
    var = _seg_sum(yc * yc, seg) * inv_n
    gn = yc * jax.lax.rsqrt(var + GN_EPS) * lng_ref[...] + lnb_ref[...]
    v = v_ref[0]
    bonus = _seg_sum(r_ref[0] * k_ref[0] * rk_ref[...], seg) * v
    rw = (gn + bonus) * g_ref[0]

    ext_ref[CONV_HALO:CONV_HALO + tt, :] = c_ref[0]
    off = CONV_HALO - (CONV_WIDTH - 1)
    conv = ext_ref[pl.ds(off, tt), :] * cw_ref[0:1, :]
    for j in range(1, CONV_WIDTH):
        conv = conv + ext_ref[pl.ds(off + j, tt), :] * cw_ref[j:j + 1, :]
    conv = conv + cb_ref[...]
    ext_ref[0:CONV_HALO, :] = ext_ref[tt:tt + CONV_HALO, :]
    mu = jnp.mean(conv, axis=-1, keepdims=True)
    cc = conv - mu
    cvar = jnp.mean(cc * cc, axis=-1, keepdims=True)
    cn = cc * jax.lax.rsqrt(cvar + LN_EPS) * clg_ref[...] + clb_ref[...]
    cs = cn * jax.nn.sigmoid(cn)

    mix = _bdot(rw, wo1_ref[...]) + _bdot(cs, wo2_ref[...])
    o_ref[0] = x_ref[0] + _rms(mix, gpost_ref[...])


def _mix(y, r, k, v, g, c, x, lng, lnb, rk, cw, cb, clg, clb, wo1, wo2, gpost, seg):
    B, T, _ = x.shape
    tt = TT_MIX
    row = lambda w: _full((1, w))
    return pl.pallas_call(
        _mix_kernel,
        grid=(B, T // tt),
        in_specs=[_tile(tt, D_RWKV)] * 6 + [_tile(tt, D_MODEL), row(D_RWKV), row(D_RWKV),
                                            row(D_RWKV), _full((CONV_HALO, D_CONV)), row(D_CONV),
                                            row(D_CONV), row(D_CONV), _full((D_RWKV, D_MODEL)),
                                            _full((D_CONV, D_MODEL)), row(D_MODEL),
                                            _full((D_RWKV, D_RWKV))],
        out_specs=_tile(tt, D_MODEL),
        out_shape=jax.ShapeDtypeStruct((B, T, D_MODEL), F32),
        scratch_shapes=[pltpu.VMEM((CONV_HALO + tt, D_CONV), F32)],
        compiler_params=_params(),
        name="mixout",
    )(y, r, k, v, g, c, x, lng, lnb, rk, cw, cb, clg, clb, wo1, wo2, gpost, seg)


def _ffn_kernel(x_ref, gpre_ref, up_ref, cw_ref, cb_ref, down_ref, gpost_ref, o_ref, carry_ref):
    t = pl.program_id(1)

    @pl.when(t == 0)
    def _():
        carry_ref[...] = jnp.zeros_like(carry_ref)

    x = x_ref[0]
    tt = x.shape[0]
    h = _rms(x, gpre_ref[...]).astype(BF16)
    acc = jnp.zeros((tt, D_MODEL), F32)
    for cblk in range(D_FF // FF_BLK):
        parts = []
        for base in (0, D_FF):
            cols = pl.ds(base + cblk * FF_BLK, FF_BLK)
            hid = _dot(h, up_ref[:, cols])
            carry = carry_ref[:, cols]
            h1 = _shift_rows(hid, carry, 1)
            h2 = _shift_rows(hid, carry, 2)
            carry_ref[:, cols] = hid[tt - SUBLANES:, :]
            parts.append(h2 * cw_ref[0:1, cols] + h1 * cw_ref[1:2, cols] + hid * cw_ref[2:3, cols]
                         + cb_ref[:, cols])
        gate, upv = parts
        act = jax.nn.gelu(gate, approximate=True) * upv
        acc = acc + _bdot(act, down_ref[pl.ds(cblk * FF_BLK, FF_BLK), :])
    o_ref[0] = x + _rms(acc, gpost_ref[...])


def _ffn(x, gpre, up, cw, cb, down, gpost):
    B, T, _ = x.shape
    tt = TT_FFN
    row = lambda w: _full((1, w))
    return pl.pallas_call(
        _ffn_kernel,
        grid=(B, T // tt),
        in_specs=[_tile(tt, D_MODEL), row(D_MODEL), _full((D_MODEL, 2 * D_FF)),
                  _full((SUBLANES, 2 * D_FF)), row(2 * D_FF), _full((D_FF, D_MODEL)),
                  row(D_MODEL)],
        out_specs=_tile(tt, D_MODEL),
        out_shape=jax.ShapeDtypeStruct((B, T, D_MODEL), F32),
        scratch_shapes=[pltpu.VMEM((SUBLANES, 2 * D_FF), F32)],
        compiler_params=_params(),
        name="convffn",
    )(x, gpre, up, cw, cb, down, gpost)


def _pad_rows(w, off, rows):
    return jnp.zeros((rows, w.shape[1]), w.dtype).at[off:off + w.shape[0]].set(w)


def _seg_ones():
    idx = np.arange(D_RWKV) // HEAD_DIM
    return jnp.asarray((idx[:, None] == idx[None, :]).astype(np.float32), dtype=BF16)


def kernel(x, w_in_first, mu_first, w_in_rest, mu_rest, vres_v0, vres_up, decay_w0, decay_up, iclr_a0, iclr_up, gate_up, k_k, k_a, r_k, lnx_g, lnx_b, cconv_w, cconv_b, cln_g, cln_b, w_out, norm_pre_mix, norm_post_mix, norm_pre_ffn, norm_post_ffn, ffn_up, ffn_conv_w, ffn_conv_b, ffn_down):
    depth = decay_w0.shape[0]
    seg = _seg_ones()
    row = lambda a: a.reshape(1, -1)
    v_first = None
    for i in range(depth):
        if i == 0:
            w_in_i, mu_i = w_in_first, mu_first
        else:
            w_in_i, mu_i = w_in_rest[i - 1], mu_rest[i - 1]
        win = jnp.pad(w_in_i, ((0, 0), (0, P_COLS - w_in_i.shape[1]))).astype(BF16)
        mu = jnp.pad(mu_i, (0, Q_COLS - mu_i.shape[0])).reshape(1, Q_COLS)
        wd = _pad_rows(decay_up[i], 0, LORA_BLK).astype(BF16)
        wa = _pad_rows(iclr_up[i], W_LORA, LORA_BLK).astype(BF16)
        wg = _pad_rows(gate_up[i], W_LORA + A_LORA, LORA_BLK).astype(BF16)
        vres = None
        if i > 0:
            wv = _pad_rows(vres_up[i - 1], W_LORA + A_LORA + G_LORA, LORA_BLK).astype(BF16)
            vres = (row(vres_v0[i - 1]), wv, v_first)
        r, lw, k, v, kk, bvec, g, c = _inproj(
            x, row(norm_pre_mix[i]), win, mu, row(decay_w0[i]), wd, row(iclr_a0[i]), wa, wg,
            row(k_k[i]), row(k_a[i]), seg, vres)
        if i == 0:
            v_first = v
        y = _wkv(r, lw, k, v, kk, bvec)
        cw = jnp.pad(cconv_w[i], ((0, CONV_HALO - CONV_WIDTH), (0, 0)))
        wo = w_out[i].astype(BF16)
        x = _mix(y, r, k, v, g, c, x, row(lnx_g[i]), row(lnx_b[i]), row(r_k[i]), cw,
                 row(cconv_b[i]), row(cln_g[i]), row(cln_b[i]), wo[:D_RWKV], wo[D_RWKV:],
                 row(norm_post_mix[i]), seg)
        fcw = jnp.pad(ffn_conv_w[i], ((0, SUBLANES - FFN_CONV_WIDTH), (0, 0)))
        x = _ffn(x, row(norm_pre_ffn[i]), ffn_up[i].astype(BF16), fcw, row(ffn_conv_b[i]),
                 ffn_down[i].astype(BF16), row(norm_post_ffn[i]))
    return x
```

```python
import functools
import math

import jax
import jax.numpy as jnp
import numpy as np
from jax.experimental import pallas as pl
from jax.experimental.pallas import tpu as pltpu

F32 = jnp.float32
BF16 = jnp.bfloat16

D_MODEL = 1024
D_RWKV = 512
D_CONV = 512
HEAD_DIM = 64
W_LORA, A_LORA, V_LORA, G_LORA = 32, 32, 32, 96
CONV_WIDTH = 31
FFN_CONV_WIDTH = 3
D_FF = 2816
GLU_COLS = 2 * D_CONV
RWKV_COLS = 3 * D_RWKV + W_LORA + A_LORA + G_LORA
RMS_EPS = 1e-6
LN_EPS = 1e-5
GN_EPS = 64e-5

LANES = 128
SUBLANES = 8
MXU_DIM = 256
LORA_OFF = 3 * D_RWKV
LORA_BLK = 2 * LANES
Q_COLS = LORA_OFF + LORA_BLK
P_COLS = GLU_COLS + Q_COLS
CHUNK = 64
GROUP = MXU_DIM
HEADS_PER_GROUP = GROUP // HEAD_DIM
N_GROUPS = D_RWKV // GROUP
CONV_HALO = 32
FF_BLK = 256
FF_ROWS = 64
FF_AHEAD = 3
VMEM_LIMIT = 56 * 1024 * 1024

TT_IN = 256
TT_WKV = 256
TT_MIX = 256
TT_FFN = 256

_HI = jax.lax.Precision.HIGHEST


def _dot(a, b, precision=None):
    return jax.lax.dot_general(a, b, (((1,), (0,)), ((), ())), precision=precision,
                               preferred_element_type=F32)


def _dot_nt(a, b, precision=None):
    return jax.lax.dot_general(a, b, (((1,), (1,)), ((), ())), precision=precision,
                               preferred_element_type=F32)


def _dot_tn(a, b, precision=None):
    return jax.lax.dot_general(a, b, (((0,), (0,)), ((), ())), precision=precision,
                               preferred_element_type=F32)


def _bdot(a, w_bf16):
    return _dot(a.astype(BF16), w_bf16)


def _seg_sum(x, seg_ones):
    hi = x.astype(BF16)
    lo = (x - hi.astype(F32)).astype(BF16)
    return _dot(hi, seg_ones) + _dot(lo, seg_ones)


def _rms(x, g):
    return x * jax.lax.rsqrt(jnp.mean(x * x, axis=-1, keepdims=True) + RMS_EPS) * g


def _shift_rows(x, carry, n):
    out = pltpu.roll(x, n, 0)
    head = out[:SUBLANES]
    row = jax.lax.broadcasted_iota(jnp.int32, head.shape, 0)
    for i in range(n):
        head = jnp.where(row == i, carry[SUBLANES - n + i:SUBLANES - n + i + 1, :], head)
    return jnp.concatenate([head, out[SUBLANES:]], axis=0)


def _inproj_kernel(has_vres, x_ref, gpre_ref, win_ref, mu_ref, w0_ref, wd_ref, a0_ref, wa_ref,
                   wg_ref, kk_ref, ka_ref, seg_ref, *rest):
    if has_vres:
        (v0_ref, wv_ref, vfirst_ref, r_out, lw_out, k_out, v_out, kk_out, b_out, g_out, c_out,
         carry_ref) = rest
    else:
        (r_out, lw_out, k_out, v_out, kk_out, b_out, g_out, c_out, carry_ref) = rest
    t = pl.program_id(1)

    @pl.when(t == 0)
    def _():
        carry_ref[...] = jnp.zeros_like(carry_ref)

    x = x_ref[0]
    h = _rms(x, gpre_ref[...])
    p = _bdot(h, win_ref[...])
    c_out[0] = p[:, :D_CONV] * jax.nn.sigmoid(p[:, D_CONV:GLU_COLS])
    q = p[:, GLU_COLS:]
    prev = _shift_rows(q, carry_ref[...], 1)
    tt = q.shape[0]
    carry_ref[...] = q[tt - SUBLANES:, :]
    q = q + (prev - q) * mu_ref[...]
    r = q[:, :D_RWKV]
    k = q[:, D_RWKV:2 * D_RWKV]
    v = q[:, 2 * D_RWKV:3 * D_RWKV]
    blk = q[:, LORA_OFF:]
    wf = w0_ref[...] + _bdot(jnp.tanh(blk), wd_ref[...])
    lw_out[0] = -math.exp(-0.5) * jax.nn.sigmoid(wf)
    a = jax.nn.sigmoid(a0_ref[...] + _bdot(blk, wa_ref[...]))
    g_out[0] = _bdot(jax.nn.sigmoid(blk), wg_ref[...])
    if has_vres:
        v = v + (vfirst_ref[0] - v) * jax.nn.sigmoid(v0_ref[...] + _bdot(blk, wv_ref[...]))
    kk = k * kk_ref[...]
    ss = _seg_sum(kk * kk, seg_ref[...])
    kk = kk * jax.lax.rsqrt(jnp.maximum(ss, 1e-24))
    r_out[0] = r
    k_out[0] = k * (1.0 + (a - 1.0) * ka_ref[...])
    v_out[0] = v
    kk_out[0] = kk
    b_out[0] = kk * a


def _full(shape):
    return pl.BlockSpec(shape, lambda b, t: (0,) * len(shape))


def _tile(tt, width):
    return pl.BlockSpec((1, tt, width), lambda b, t: (b, t, 0))


def _params():
    return pltpu.CompilerParams(dimension_semantics=("arbitrary", "arbitrary"),
                                vmem_limit_bytes=VMEM_LIMIT)


def _inproj(x, gpre, win, mu, w0, wd, a0, wa, wg, k_k, k_a, seg, vres):
    B, T, _ = x.shape
    tt = TT_IN
    has_vres = vres is not None
    row = lambda w: _full((1, w))
    in_specs = [_tile(tt, D_MODEL), row(D_MODEL), _full((D_MODEL, P_COLS)), row(Q_COLS),
                row(D_RWKV), _full((LORA_BLK, D_RWKV)), row(D_RWKV), _full((LORA_BLK, D_RWKV)),
                _full((LORA_BLK, D_RWKV)), row(D_RWKV), row(D_RWKV), _full((D_RWKV, D_RWKV))]
    args = [x, gpre, win, mu, w0, wd, a0, wa, wg, k_k, k_a, seg]
    if has_vres:
        v0, wv, vfirst = vres
        in_specs += [row(D_RWKV), _full((LORA_BLK, D_RWKV)), _tile(tt, D_RWKV)]
        args += [v0, wv, vfirst]
    out = jax.ShapeDtypeStruct((B, T, D_RWKV), F32)
    return pl.pallas_call(
        functools.partial(_inproj_kernel, has_vres),
        grid=(B, T // tt),
        in_specs=in_specs,
        out_specs=[_tile(tt, D_RWKV)] * 8,
        out_shape=[out] * 8,
        scratch_shapes=[pltpu.VMEM((SUBLANES, Q_COLS), F32)],
        compiler_params=_params(),
        name="inproj",
    )(*args)


def _wkv_masks():
    i = np.arange(GROUP)
    same = (i[:, None] // CHUNK) == (i[None, :] // HEAD_DIM)
    rin, cin = i[:, None] % CHUNK, i[None, :] % CHUNK
    m = np.stack([same, same & (rin > cin), same & (rin >= cin), i[:, None] == i[None, :]])
    t = np.arange(CHUNK)
    return (jnp.asarray(m.astype(np.float32), dtype=BF16),
            jnp.asarray((t[:, None] >= t[None, :]).astype(np.float32), dtype=BF16))


def _wkv_kernel(r_ref, lw_ref, k_ref, v_ref, kk_ref, b_ref, mask_ref, tri_ref, y_ref, z_ref):
    t = pl.program_id(1)

    @pl.when(t == 0)
    def _():
        z_ref[...] = jnp.zeros_like(z_ref)

    def bd(xl):
        return jnp.concatenate([xl.astype(BF16)] * HEADS_PER_GROUP, axis=0) * mask_ref[0]

    def fold(xb):
        out = xb[0:CHUNK]
        for hh in range(1, HEADS_PER_GROUP):
            out = out + xb[hh * CHUNK:(hh + 1) * CHUNK]
        return out

    def masked(x, which):
        return x.astype(BF16) * mask_ref[which]

    n_chunks = r_ref.shape[1] // CHUNK

    insts = [(c, gi) for c in range(n_chunks) for gi in range(N_GROUPS)]
    eye = mask_ref[3]
    st = []
    for c, gi in insts:
        rows, lanes = pl.ds(c * CHUNK, CHUNK), pl.ds(gi * GROUP, GROUP)
        lw = lw_ref[0, rows, lanes]
        tri = tri_ref[...]
        p1 = lw.astype(BF16)
        r1 = lw - p1.astype(F32)
        p2 = r1.astype(BF16)
        p3 = (r1 - p2.astype(F32)).astype(BF16)
        G = _dot(tri, p1) + _dot(tri, p2) + _dot(tri, p3)
        glast = G[CHUNK - 1:CHUNK, :]
        eG, eGp, enG, eGC = jnp.exp(G), jnp.exp(G - lw), jnp.exp(-G), jnp.exp(glast - G)
        r, k, v = r_ref[0, rows, lanes], k_ref[0, rows, lanes], v_ref[0, rows, lanes]
        a, b = -kk_ref[0, rows, lanes], b_ref[0, rows, lanes]
        rt = r * eG
        am, rm, bm, km, vm = bd(a * eGp), bd(rt), bd(b * enG), bd(k * enG), bd(v)
        a_ab = masked(_dot_nt(am, bm), 1)
        s = dict(rt=rt, am=am, vm=vm, a_ab=a_ab, ak=a_ab, tinv=a_ab + eye,
                 a_ak=masked(_dot_nt(am, km), 1), a_rb=masked(_dot_nt(rm, bm), 2),
                 a_rk=masked(_dot_nt(rm, km), 2),
                 bht=(b * eGC).T.astype(BF16), kht=(k * eGC).T.astype(BF16),
                 vb=v.astype(BF16), pc=jnp.exp(glast))
        st.append(s)
    for _ in range(int(math.log2(CHUNK)) - 1):
        for s in st:
            s["ak"] = _dot(s["ak"], s["ak"]).astype(BF16)
        for s in st:
            s["tinv"] = _dot(s["tinv"], s["ak"] + eye).astype(BF16)
    for s in st:
        s["av"] = _dot(s["a_ak"], s["vm"]).astype(BF16)
    for s in st:
        s["u0"] = _dot(s["tinv"], s["av"])
        s["wt"] = _dot(s["tinv"], s["am"])
    same = mask_ref[0].astype(F32)
    for s in st:
        u0, wt = s["u0"], s["wt"]
        s["y0"] = fold(_dot(s["a_rk"], s["vm"]) + _dot(s["a_rb"], u0.astype(BF16)))
        s["rbar"] = (s["rt"] + fold(_dot(s["a_rb"], wt.astype(BF16)))).astype(BF16)
        m = _dot(s["bht"], fold(wt).astype(BF16)) * same + eye.astype(F32) * s["pc"]
        s["m"] = m.astype(BF16)
        s["q"] = (_dot(s["bht"], fold(u0).astype(BF16)) + _dot(s["kht"], s["vb"])) * same
    z = [z_ref[gi] for gi in range(N_GROUPS)]
    for (c, gi), s in zip(insts, st):
        rows, lanes = pl.ds(c * CHUNK, CHUNK), pl.ds(gi * GROUP, GROUP)
        z0 = z[gi].astype(BF16)
        y_ref[0, rows, lanes] = s["y0"] + _dot(s["rbar"], z0)
        z[gi] = _dot(s["m"], z0) + s["q"]
    for gi in range(N_GROUPS):
        z_ref[gi] = z[gi]


def _wkv(r, lw, k, v, kk, bvec):
    B, T, _ = r.shape
    tt = TT_WKV
    masks, tri = _wkv_masks()
    return pl.pallas_call(
        _wkv_kernel,
        grid=(B, T // tt),
        in_specs=[_tile(tt, D_RWKV)] * 6 + [_full(masks.shape), _full(tri.shape)],
        out_specs=_tile(tt, D_RWKV),
        out_shape=jax.ShapeDtypeStruct((B, T, D_RWKV), F32),
        scratch_shapes=[pltpu.VMEM((N_GROUPS, GROUP, GROUP), F32)],
        compiler_params=_params(),
        name="wkv",
    )(r, lw, k, v, kk, bvec, masks, tri)


def _mix_kernel(y_ref, r_ref, k_ref, v_ref, g_ref, c_ref, x_ref, lng_ref, lnb_ref, rk_ref,
                cw_ref, cb_ref, clg_ref, clb_ref, wo1_ref, wo2_ref, gpost_ref, seg_ref,
                o_ref, ext_ref):
    t = pl.program_id(1)
    tt = c_ref.shape[1]

    @pl.when(t == 0)
    def _():
        ext_ref[0:CONV_HALO, :] = jnp.zeros((CONV_HALO, D_CONV), F32)

    seg = seg_ref[...]
    inv_n = 1.0 / HEAD_DIM
    y = y_ref[0]
    mean = _seg_sum(y, seg) * inv_n
    yc = y - mean
    var = _seg_sum(yc * yc, seg) * inv_n
    gn = yc * jax.lax.rsqrt(var + GN_EPS) * lng_ref[...] + lnb_ref[...]
    v = v_ref[0]
    bonus = _seg_sum(r_ref[0] * k_ref[0] * rk_ref[...], seg) * v
    rw = (gn + bonus) * g_ref[0]

    ext_ref[CONV_HALO:CONV_HALO + tt, :] = c_ref[0]
    off = CONV_HALO - (CONV_WIDTH - 1)
    conv = ext_ref[pl.ds(off, tt), :] * cw_ref[0:1, :]
    for j in range(1, CONV_WIDTH):
        conv = conv + ext_ref[pl.ds(off + j, tt), :] * cw_ref[j:j + 1, :]
    conv = conv + cb_ref[...]
    ext_ref[0:CONV_HALO, :] = ext_ref[tt:tt + CONV_HALO, :]
    mu = jnp.mean(conv, axis=-1, keepdims=True)
    cc = conv - mu
    cvar = jnp.mean(cc * cc, axis=-1, keepdims=True)
    cn = cc * jax.lax.rsqrt(cvar + LN_EPS) * clg_ref[...] + clb_ref[...]
    cs = cn * jax.nn.sigmoid(cn)

    mix = _bdot(rw, wo1_ref[...]) + _bdot(cs, wo2_ref[...])
    o_ref[0] = x_ref[0] + _rms(mix, gpost_ref[...])


def _mix(y, r, k, v, g, c, x, lng, lnb, rk, cw, cb, clg, clb, wo1, wo2, gpost, seg):
    B, T, _ = x.shape
    tt = TT_MIX
    row = lambda w: _full((1, w))
    return pl.pallas_call(
        _mix_kernel,
        grid=(B, T // tt),
        in_specs=[_tile(tt, D_RWKV)] * 6 + [_tile(tt, D_MODEL), row(D_RWKV), row(D_RWKV),
                                            row(D_RWKV), _full((CONV_HALO, D_CONV)), row(D_CONV),
                                            row(D_CONV), row(D_CONV), _full((D_RWKV, D_MODEL)),
                                            _full((D_CONV, D_MODEL)), row(D_MODEL),
                                            _full((D_RWKV, D_RWKV))],
        out_specs=_tile(tt, D_MODEL),
        out_shape=jax.ShapeDtypeStruct((B, T, D_MODEL), F32),
        scratch_shapes=[pltpu.VMEM((CONV_HALO + tt, D_CONV), F32)],
        compiler_params=_params(),
        name="mixout",
    )(y, r, k, v, g, c, x, lng, lnb, rk, cw, cb, clg, clb, wo1, wo2, gpost, seg)


def _ffn_kernel(x_ref, gpre_ref, up_ref, cw_ref, cb_ref, down_ref, gpost_ref, o_ref, carry_ref):
    t = pl.program_id(1)

    @pl.when(t == 0)
    def _():
        carry_ref[...] = jnp.zeros_like(carry_ref)

    x = x_ref[0]
    tt = x.shape[0]
    h = _rms(x, gpre_ref[...]).astype(BF16)
    n_blk = D_FF // FF_BLK

    def up_proj(cblk):
        return [_dot(h, up_ref[:, pl.ds(base + cblk * FF_BLK, FF_BLK)]) for base in (0, D_FF)]

    def conv(hid, carry, cols):
        h1 = _shift_rows(hid, carry, 1)
        h2 = _shift_rows(hid, carry, 2)
        return (h2 * cw_ref[0:1, cols] + h1 * cw_ref[1:2, cols] + hid * cw_ref[2:3, cols]
                + cb_ref[:, cols])

    def glu(hid, cblk):
        cg, cu = pl.ds(cblk * FF_BLK, FF_BLK), pl.ds(D_FF + cblk * FF_BLK, FF_BLK)
        hg, hu = hid
        carry_g, carry_u = carry_ref[:, cg], carry_ref[:, cu]
        carry_ref[:, cg] = hg[tt - SUBLANES:, :]
        carry_ref[:, cu] = hu[tt - SUBLANES:, :]
        acts = []
        for r0 in range(0, tt, FF_ROWS):
            if r0:
                carry_g, carry_u = hg[r0 - SUBLANES:r0], hu[r0 - SUBLANES:r0]
            gate = conv(hg[r0:r0 + FF_ROWS], carry_g, cg)
            upv = conv(hu[r0:r0 + FF_ROWS], carry_u, cu)
            acts.append((jax.nn.gelu(gate, approximate=True) * upv).astype(BF16))
        return jnp.concatenate(acts, axis=0)

    acc = jnp.zeros((tt, D_MODEL), F32)
    hids = [up_proj(c) for c in range(min(FF_AHEAD, n_blk))]
    for cblk in range(n_blk):
        if cblk + FF_AHEAD < n_blk:
            hids.append(up_proj(cblk + FF_AHEAD))
        acc = acc + _dot(glu(hids[cblk], cblk), down_ref[pl.ds(cblk * FF_BLK, FF_BLK), :])
    o_ref[0] = x + _rms(acc, gpost_ref[...])


def _ffn(x, gpre, up, cw, cb, down, gpost):
    B, T, _ = x.shape
    tt = TT_FFN
    row = lambda w: _full((1, w))
    return pl.pallas_call(
        _ffn_kernel,
        grid=(B, T // tt),
        in_specs=[_tile(tt, D_MODEL), row(D_MODEL), _full((D_MODEL, 2 * D_FF)),
                  _full((SUBLANES, 2 * D_FF)), row(2 * D_FF), _full((D_FF, D_MODEL)),
                  row(D_MODEL)],
        out_specs=_tile(tt, D_MODEL),
        out_shape=jax.ShapeDtypeStruct((B, T, D_MODEL), F32),
        scratch_shapes=[pltpu.VMEM((SUBLANES, 2 * D_FF), F32)],
        compiler_params=_params(),
        name="convffn",
    )(x, gpre, up, cw, cb, down, gpost)


def _pad_rows(w, off, rows):
    return jnp.zeros((rows, w.shape[1]), w.dtype).at[off:off + w.shape[0]].set(w)


def _seg_ones():
    idx = np.arange(D_RWKV) // HEAD_DIM
    return jnp.asarray((idx[:, None] == idx[None, :]).astype(np.float32), dtype=BF16)


def kernel(x, w_in_first, mu_first, w_in_rest, mu_rest, vres_v0, vres_up, decay_w0, decay_up, iclr_a0, iclr_up, gate_up, k_k, k_a, r_k, lnx_g, lnx_b, cconv_w, cconv_b, cln_g, cln_b, w_out, norm_pre_mix, norm_post_mix, norm_pre_ffn, norm_post_ffn, ffn_up, ffn_conv_w, ffn_conv_b, ffn_down):
    depth = decay_w0.shape[0]
    seg = _seg_ones()
    row = lambda a: a.reshape(1, -1)
    v_first = None
    for i in range(depth):
        if i == 0:
            w_in_i, mu_i = w_in_first, mu_first
        else:
            w_in_i, mu_i = w_in_rest[i - 1], mu_rest[i - 1]
        win = jnp.pad(w_in_i, ((0, 0), (0, P_COLS - w_in_i.shape[1]))).astype(BF16)
        mu = jnp.pad(mu_i, (0, Q_COLS - mu_i.shape[0])).reshape(1, Q_COLS)
        wd = _pad_rows(decay_up[i], 0, LORA_BLK).astype(BF16)
        wa = _pad_rows(iclr_up[i], W_LORA, LORA_BLK).astype(BF16)
        wg = _pad_rows(gate_up[i], W_LORA + A_LORA, LORA_BLK).astype(BF16)
        vres = None
        if i > 0:
            wv = _pad_rows(vres_up[i - 1], W_LORA + A_LORA + G_LORA, LORA_BLK).astype(BF16)
            vres = (row(vres_v0[i - 1]), wv, v_first)
        r, lw, k, v, kk, bvec, g, c = _inproj(
            x, row(norm_pre_mix[i]), win, mu, row(decay_w0[i]), wd, row(iclr_a0[i]), wa, wg,
            row(k_k[i]), row(k_a[i]), seg, vres)
        if i == 0:
            v_first = v
        y = _wkv(r, lw, k, v, kk, bvec)
        cw = jnp.pad(cconv_w[i], ((0, CONV_HALO - CONV_WIDTH), (0, 0)))
        wo = w_out[i].astype(BF16)
        x = _mix(y, r, k, v, g, c, x, row(lnx_g[i]), row(lnx_b[i]), row(r_k[i]), cw,
                 row(cconv_b[i]), row(cln_g[i]), row(cln_b[i]), wo[:D_RWKV], wo[D_RWKV:],
                 row(norm_post_mix[i]), seg)
        fcw = jnp.pad(ffn_conv_w[i], ((0, SUBLANES - FFN_CONV_WIDTH), (0, 0)))
        x = _ffn(x, row(norm_pre_ffn[i]), ffn_up[i].astype(BF16), fcw, row(ffn_conv_b[i]),
                 ffn_down[i].astype(BF16), row(norm_post_ffn[i]))
    return x
```

```python
import functools
import math

import jax
import jax.numpy as jnp
import numpy as np
from jax.experimental import pallas as pl
from jax.experimental.pallas import tpu as pltpu

F32 = jnp.float32
BF16 = jnp.bfloat16

D_MODEL = 1024
D_RWKV = 512
D_CONV = 512
HEAD_DIM = 64
W_LORA, A_LORA, V_LORA, G_LORA = 32, 32, 32, 96
CONV_WIDTH = 31
FFN_CONV_WIDTH = 3
D_FF = 2816
GLU_COLS = 2 * D_CONV
RWKV_COLS = 3 * D_RWKV + W_LORA + A_LORA + G_LORA
RMS_EPS = 1e-6
LN_EPS = 1e-5
GN_EPS = 64e-5

LANES = 128
SUBLANES = 8
MXU_DIM = 256
LORA_OFF = 3 * D_RWKV
LORA_BLK = 2 * LANES
Q_COLS = LORA_OFF + LORA_BLK
P_COLS = GLU_COLS + Q_COLS
CHUNK = 64
GROUP = MXU_DIM
HEADS_PER_GROUP = GROUP // HEAD_DIM
N_GROUPS = D_RWKV // GROUP
CONV_HALO = 32
CONV_ROWS = 32
FF_BLK = 256
FF_ROWS = 64
FF_AHEAD = 3
VMEM_LIMIT = 56 * 1024 * 1024

TT_IN = 256
TT_WKV = 256
TT_FFN = 256


def _dot(a, b):
    return jax.lax.dot_general(a, b, (((1,), (0,)), ((), ())), preferred_element_type=F32)


def _dot_nt(a, b):
    return jax.lax.dot_general(a, b, (((1,), (1,)), ((), ())), preferred_element_type=F32)


def _bdot(a, w_bf16):
    return _dot(a.astype(BF16), w_bf16)


def _seg_sum(x, seg_ones):
    xb = x.astype(BF16)
    return jnp.concatenate([_dot(xb[:, i:i + MXU_DIM], seg_ones)
                            for i in range(0, x.shape[1], MXU_DIM)], axis=1)


def _rms(x, g):
    return x * jax.lax.rsqrt(jnp.mean(x * x, axis=-1, keepdims=True) + RMS_EPS) * g


def _shift_rows(x, carry, n):
    out = pltpu.roll(x, n, 0)
    head = out[:SUBLANES]
    row = jax.lax.broadcasted_iota(jnp.int32, head.shape, 0)
    for i in range(n):
        head = jnp.where(row == i, carry[SUBLANES - n + i:SUBLANES - n + i + 1, :], head)
    return jnp.concatenate([head, out[SUBLANES:]], axis=0)


def _inproj_kernel(has_vres, x_ref, gpre_ref, win_ref, mu_ref, w0_ref, wd_ref, a0_ref, wa_ref,
                   wg_ref, kk_ref, ka_ref, seg_ref, *rest):
    if has_vres:
        (v0_ref, wv_ref, vfirst_ref, r_out, lw_out, k_out, v_out, kk_out, b_out, g_out, c_out,
         carry_ref) = rest
    else:
        (r_out, lw_out, k_out, v_out, kk_out, b_out, g_out, c_out, carry_ref) = rest
    t = pl.program_id(1)

    @pl.when(t == 0)
    def _():
        carry_ref[...] = jnp.zeros_like(carry_ref)

    x = x_ref[0]
    h = _rms(x, gpre_ref[...])
    p = _bdot(h, win_ref[...])
    c_out[0] = p[:, :D_CONV] * jax.nn.sigmoid(p[:, D_CONV:GLU_COLS])
    q = p[:, GLU_COLS:]
    prev = _shift_rows(q, carry_ref[...], 1)
    tt = q.shape[0]
    carry_ref[...] = q[tt - SUBLANES:, :]
    q = q + (prev - q) * mu_ref[...]
    r = q[:, :D_RWKV]
    k = q[:, D_RWKV:2 * D_RWKV]
    v = q[:, 2 * D_RWKV:3 * D_RWKV]
    blk = q[:, LORA_OFF:]
    wf = w0_ref[...] + _bdot(jnp.tanh(blk), wd_ref[...])
    lw_out[0] = -math.exp(-0.5) * jax.nn.sigmoid(wf)
    a = jax.nn.sigmoid(a0_ref[...] + _bdot(blk, wa_ref[...]))
    g_out[0] = _bdot(jax.nn.sigmoid(blk), wg_ref[...])
    if has_vres:
        v = v + (vfirst_ref[0] - v) * jax.nn.sigmoid(v0_ref[...] + _bdot(blk, wv_ref[...]))
    kk = k * kk_ref[...]
    ss = _seg_sum(kk * kk, seg_ref[...])
    kk = kk * jax.lax.rsqrt(jnp.maximum(ss, 1e-24))
    r_out[0] = r
    k_out[0] = k * (1.0 + (a - 1.0) * ka_ref[...])
    v_out[0] = v
    kk_out[0] = kk
    b_out[0] = kk * a


def _full(shape):
    return pl.BlockSpec(shape, lambda b, t: (0,) * len(shape))


def _tile(tt, width):
    return pl.BlockSpec((1, tt, width), lambda b, t: (b, t, 0))


def _params():
    return pltpu.CompilerParams(dimension_semantics=("arbitrary", "arbitrary"),
                                vmem_limit_bytes=VMEM_LIMIT)


def _inproj(x, gpre, win, mu, w0, wd, a0, wa, wg, k_k, k_a, seg, vres):
    B, T, _ = x.shape
    tt = TT_IN
    has_vres = vres is not None
    row = lambda w: _full((1, w))
    in_specs = [_tile(tt, D_MODEL), row(D_MODEL), _full((D_MODEL, P_COLS)), row(Q_COLS),
                row(D_RWKV), _full((LORA_BLK, D_RWKV)), row(D_RWKV), _full((LORA_BLK, D_RWKV)),
                _full((LORA_BLK, D_RWKV)), row(D_RWKV), row(D_RWKV), _full((MXU_DIM, MXU_DIM))]
    args = [x, gpre, win, mu, w0, wd, a0, wa, wg, k_k, k_a, seg]
    if has_vres:
        v0, wv, vfirst = vres
        in_specs += [row(D_RWKV), _full((LORA_BLK, D_RWKV)), _tile(tt, D_RWKV)]
        args += [v0, wv, vfirst]
    out = jax.ShapeDtypeStruct((B, T, D_RWKV), F32)
    return pl.pallas_call(
        functools.partial(_inproj_kernel, has_vres),
        grid=(B, T // tt),
        in_specs=in_specs,
        out_specs=[_tile(tt, D_RWKV)] * 8,
        out_shape=[out] * 8,
        scratch_shapes=[pltpu.VMEM((SUBLANES, Q_COLS), F32)],
        compiler_params=_params(),
        name="inproj",
    )(*args)


def _wkv_masks():
    i = np.arange(GROUP)
    same = (i[:, None] // CHUNK) == (i[None, :] // HEAD_DIM)
    rin, cin = i[:, None] % CHUNK, i[None, :] % CHUNK
    m = np.stack([same, same & (rin > cin), same & (rin >= cin), i[:, None] == i[None, :]])
    t = np.arange(CHUNK)
    return (jnp.asarray(m.astype(np.float32), dtype=BF16),
            jnp.asarray((t[:, None] >= t[None, :]).astype(np.float32), dtype=BF16))


def _wkvmix_kernel(r_ref, lw_ref, k_ref, v_ref, kk_ref, b_ref, g_ref, c_ref, x_ref, mask_ref,
                   tri_ref, lng_ref, lnb_ref, rk_ref, cw_ref, cb_ref, clg_ref, clb_ref, wo1_ref,
                   wo2_ref, gpost_ref, seg_ref, o_ref, z_ref, ext_ref, sh_ref, acc_ref, ak_ref,
                   tinv_ref):
    t = pl.program_id(1)
    tt = c_ref.shape[1]

    @pl.when(t == 0)
    def _():
        z_ref[...] = jnp.zeros_like(z_ref)
        ext_ref[0:CONV_HALO, :] = jnp.zeros((CONV_HALO, D_CONV), F32)

    def bd(xl):
        return jnp.concatenate([xl.astype(BF16)] * HEADS_PER_GROUP, axis=0) * mask_ref[0]

    def fold(xb):
        out = xb[0:CHUNK]
        for hh in range(1, HEADS_PER_GROUP):
            out = out + xb[hh * CHUNK:(hh + 1) * CHUNK]
        return out

    def masked(x, which):
        return x.astype(BF16) * mask_ref[which]

    ext_ref[CONV_HALO:CONV_HALO + tt, :] = c_ref[0]
    span = tt + CONV_HALO - SUBLANES
    sh_ref[0] = ext_ref[...]
    for b in range(1, SUBLANES):
        sh_ref[b, 0:span, :] = ext_ref[pl.ds(b, span), :]
    ext_ref[0:CONV_HALO, :] = ext_ref[tt:tt + CONV_HALO, :]
    off = CONV_HALO - (CONV_WIDTH - 1)
    acc_ref[...] = (sh_ref[off % SUBLANES, pl.ds(off // SUBLANES * SUBLANES, tt), :] * cw_ref[0:1, :]
                    + cb_ref[...])

    n_chunks = tt // CHUNK
    insts = [(c, gi) for c in range(n_chunks) for gi in range(N_GROUPS)]
    eye = mask_ref[3]
    st = []
    tri = tri_ref[...]
    for c, gi in insts:
        rows, lanes = pl.ds(c * CHUNK, CHUNK), pl.ds(gi * GROUP, GROUP)
        lw = lw_ref[0, rows, lanes]
        p1 = lw.astype(BF16)
        r1 = lw - p1.astype(F32)
        p2 = r1.astype(BF16)
        p3 = (r1 - p2.astype(F32)).astype(BF16)
        st.append(dict(lw=lw, G=_dot(tri, p1) + _dot(tri, p2) + _dot(tri, p3)))
    for (c, gi), s in zip(insts, st):
        rows, lanes = pl.ds(c * CHUNK, CHUNK), pl.ds(gi * GROUP, GROUP)
        G, lw = s["G"], s["lw"]
        glast = G[CHUNK - 1:CHUNK, :]
        eG, eGp, enG, eGC = jnp.exp(G), jnp.exp(G - lw), jnp.exp(-G), jnp.exp(glast - G)
        r, k, v = r_ref[0, rows, lanes], k_ref[0, rows, lanes], v_ref[0, rows, lanes]
        a, b = -kk_ref[0, rows, lanes], b_ref[0, rows, lanes]
        rt = r * eG
        s.update(rt=rt, am=bd(a * eGp), rm=bd(rt), bm=bd(b * enG), km=bd(k * enG), vm=bd(v),
                 bht=(b * eGC).T.astype(BF16), kht=(k * eGC).T.astype(BF16),
                 vb=v.astype(BF16), pc=jnp.exp(glast))
    for i, s in enumerate(st):
        a_ab = masked(_dot_nt(s["am"], s["bm"]), 1)
        ak_ref[i] = a_ab
        tinv_ref[i] = a_ab + eye
    for s in st:
        s["a_ak"] = masked(_dot_nt(s["am"], s["km"]), 1)
    for s in st:
        s["a_rb"] = masked(_dot_nt(s["rm"], s["bm"]), 2)
    for s in st:
        last = _dot_nt(s["rm"], s["km"])
        s["a_rk"] = masked(last, 2)

    def zero_after(x):
        u = jax.lax.bitcast_convert_type(x[0:SUBLANES, 0:LANES], jnp.uint32)
        z = jax.lax.bitcast_convert_type((u >> 16) >> 16, F32)
        z = jnp.concatenate([z] * (D_CONV // LANES), axis=1)
        return jnp.concatenate([z] * (CONV_ROWS // SUBLANES), axis=0)

    n_sq = int(math.log2(CHUNK)) - 1
    taps_per_step = (CONV_WIDTH - 1) // n_sq
    assert taps_per_step * n_sq == CONV_WIDTH - 1
    for it in range(n_sq):
        zero = zero_after(last)
        for r0 in range(0, tt, CONV_ROWS):
            acc = acc_ref[r0:r0 + CONV_ROWS, :] + zero
            for j in range(1 + it * taps_per_step, 1 + (it + 1) * taps_per_step):
                a, b = divmod(off + j, SUBLANES)
                acc = acc + sh_ref[b, pl.ds(a * SUBLANES + r0, CONV_ROWS), :] * cw_ref[j:j + 1, :]
            acc_ref[r0:r0 + CONV_ROWS, :] = acc
        for i in range(len(insts)):
            ak = ak_ref[i]
            ak_ref[i] = _dot(ak, ak).astype(BF16)
        for i in range(len(insts)):
            last = _dot(tinv_ref[i], ak_ref[i] + eye)
            tinv_ref[i] = last.astype(BF16)
    for i, s in enumerate(st):
        s["tinv"] = tinv_ref[i]
    for s in st:
        s["av"] = _dot(s["a_ak"], s["vm"]).astype(BF16)
    for s in st:
        s["u0"] = _dot(s["tinv"], s["av"])
        s["wt"] = _dot(s["tinv"], s["am"])

    same = mask_ref[0].astype(F32)
    for s in st:
        u0, wt = s["u0"], s["wt"]
        s["y0"] = fold(_dot(s["a_rk"], s["vm"]) + _dot(s["a_rb"], u0.astype(BF16)))
        s["rbar"] = (s["rt"] + fold(_dot(s["a_rb"], wt.astype(BF16)))).astype(BF16)
        m = _dot(s["bht"], fold(wt).astype(BF16)) * same + eye.astype(F32) * s["pc"]
        s["m"] = m.astype(BF16)
        s["q"] = (_dot(s["bht"], fold(u0).astype(BF16)) + _dot(s["kht"], s["vb"])) * same
    conv = acc_ref[...]
    mu = jnp.mean(conv, axis=-1, keepdims=True)
    cc = conv - mu
    cvar = jnp.mean(cc * cc, axis=-1, keepdims=True)
    cn = cc * jax.lax.rsqrt(cvar + LN_EPS) * clg_ref[...] + clb_ref[...]
    mix_c = _bdot(cn * jax.nn.sigmoid(cn), wo2_ref[...])
    z = [z_ref[gi] for gi in range(N_GROUPS)]
    ys = []
    for (c, gi), s in zip(insts, st):
        z0 = z[gi].astype(BF16)
        ys.append(s["y0"] + _dot(s["rbar"], z0))
        z[gi] = _dot(s["m"], z0) + s["q"]
    for gi in range(N_GROUPS):
        z_ref[gi] = z[gi]
    y = jnp.concatenate([jnp.concatenate(ys[c * N_GROUPS:(c + 1) * N_GROUPS], axis=1)
                         for c in range(n_chunks)], axis=0)

    seg = seg_ref[...]
    inv_n = 1.0 / HEAD_DIM
    mean = _seg_sum(y, seg) * inv_n
    yc = y - mean
    var = _seg_sum(yc * yc, seg) * inv_n
    gn = yc * jax.lax.rsqrt(var + GN_EPS) * lng_ref[...] + lnb_ref[...]
    bonus = _seg_sum(r_ref[0] * k_ref[0] * rk_ref[...], seg) * v_ref[0]
    rw = (gn + bonus) * g_ref[0]
    mix = _bdot(rw, wo1_ref[...]) + mix_c
    o_ref[0] = x_ref[0] + _rms(mix, gpost_ref[...])


def _wkvmix(r, lw, k, v, kk, bvec, g, c, x, lng, lnb, rk, cw, cb, clg, clb, wo1, wo2, gpost, seg):
    B, T, _ = x.shape
    tt = TT_WKV
    n_inst = tt // CHUNK * N_GROUPS
    masks, tri = _wkv_masks()
    row = lambda w: _full((1, w))
    return pl.pallas_call(
        _wkvmix_kernel,
        grid=(B, T // tt),
        in_specs=[_tile(tt, D_RWKV)] * 8 + [_tile(tt, D_MODEL), _full(masks.shape),
                                            _full(tri.shape), row(D_RWKV), row(D_RWKV),
                                            row(D_RWKV), _full((CONV_HALO, D_CONV)), row(D_CONV),
                                            row(D_CONV), row(D_CONV), _full((D_RWKV, D_MODEL)),
                                            _full((D_CONV, D_MODEL)), row(D_MODEL),
                                            _full((MXU_DIM, MXU_DIM))],
        out_specs=_tile(tt, D_MODEL),
        out_shape=jax.ShapeDtypeStruct((B, T, D_MODEL), F32),
        scratch_shapes=[pltpu.VMEM((N_GROUPS, GROUP, GROUP), F32),
                        pltpu.VMEM((CONV_HALO + tt, D_CONV), F32),
                        pltpu.VMEM((SUBLANES, CONV_HALO + tt, D_CONV), F32),
                        pltpu.VMEM((tt, D_CONV), F32),
                        pltpu.VMEM((n_inst, GROUP, GROUP), BF16),
                        pltpu.VMEM((n_inst, GROUP, GROUP), BF16)],
        compiler_params=_params(),
        name="wkvmix",
    )(r, lw, k, v, kk, bvec, g, c, x, masks, tri, lng, lnb, rk, cw, cb, clg, clb, wo1, wo2,
      gpost, seg)


def _ffn_kernel(x_ref, gpre_ref, up_ref, cw_ref, cb_ref, down_ref, gpost_ref, o_ref, carry_ref):
    t = pl.program_id(1)

    @pl.when(t == 0)
    def _():
        carry_ref[...] = jnp.zeros_like(carry_ref)

    x = x_ref[0]
    tt = x.shape[0]
    h = _rms(x, gpre_ref[...]).astype(BF16)
    n_blk = D_FF // FF_BLK

    def up_proj(cblk):
        return [_dot(h, up_ref[:, pl.ds(base + cblk * FF_BLK, FF_BLK)]) for base in (0, D_FF)]

    def conv(hid, carry, cols):
        h1 = _shift_rows(hid, carry, 1)
        h2 = _shift_rows(hid, carry, 2)
        return (h2 * cw_ref[0:1, cols] + h1 * cw_ref[1:2, cols] + hid * cw_ref[2:3, cols]
                + cb_ref[:, cols])

    def glu(hid, cblk):
        cg, cu = pl.ds(cblk * FF_BLK, FF_BLK), pl.ds(D_FF + cblk * FF_BLK, FF_BLK)
        hg, hu = hid
        carry_g, carry_u = carry_ref[:, cg], carry_ref[:, cu]
        carry_ref[:, cg] = hg[tt - SUBLANES:, :]
        carry_ref[:, cu] = hu[tt - SUBLANES:, :]
        acts = []
        for r0 in range(0, tt, FF_ROWS):
            if r0:
                carry_g, carry_u = hg[r0 - SUBLANES:r0], hu[r0 - SUBLANES:r0]
            gate = conv(hg[r0:r0 + FF_ROWS], carry_g, cg)
            upv = conv(hu[r0:r0 + FF_ROWS], carry_u, cu)
            acts.append((jax.nn.gelu(gate, approximate=True) * upv).astype(BF16))
        return jnp.concatenate(acts, axis=0)

    acc = jnp.zeros((tt, D_MODEL), F32)
    hids = [up_proj(c) for c in range(min(FF_AHEAD, n_blk))]
    for cblk in range(n_blk):
        if cblk + FF_AHEAD < n_blk:
            hids.append(up_proj(cblk + FF_AHEAD))
        acc = acc + _dot(glu(hids[cblk], cblk), down_ref[pl.ds(cblk * FF_BLK, FF_BLK), :])
    o_ref[0] = x + _rms(acc, gpost_ref[...])


def _ffn(x, gpre, up, cw, cb, down, gpost):
    B, T, _ = x.shape
    tt = TT_FFN
    row = lambda w: _full((1, w))
    return pl.pallas_call(
        _ffn_kernel,
        grid=(B, T // tt),
        in_specs=[_tile(tt, D_MODEL), row(D_MODEL), _full((D_MODEL, 2 * D_FF)),
                  _full((SUBLANES, 2 * D_FF)), row(2 * D_FF), _full((D_FF, D_MODEL)),
                  row(D_MODEL)],
        out_specs=_tile(tt, D_MODEL),
        out_shape=jax.ShapeDtypeStruct((B, T, D_MODEL), F32),
        scratch_shapes=[pltpu.VMEM((SUBLANES, 2 * D_FF), F32)],
        compiler_params=_params(),
        name="convffn",
    )(x, gpre, up, cw, cb, down, gpost)


def _pad_rows(w, off, rows):
    return jnp.zeros((rows, w.shape[1]), w.dtype).at[off:off + w.shape[0]].set(w)


def _seg_ones():
    idx = np.arange(MXU_DIM) // HEAD_DIM
    return jnp.asarray((idx[:, None] == idx[None, :]).astype(np.float32), dtype=BF16)


def kernel(x, w_in_first, mu_first, w_in_rest, mu_rest, vres_v0, vres_up, decay_w0, decay_up, iclr_a0, iclr_up, gate_up, k_k, k_a, r_k, lnx_g, lnx_b, cconv_w, cconv_b, cln_g, cln_b, w_out, norm_pre_mix, norm_post_mix, norm_pre_ffn, norm_post_ffn, ffn_up, ffn_conv_w, ffn_conv_b, ffn_down):
    depth = decay_w0.shape[0]
    seg = _seg_ones()
    row = lambda a: a.reshape(1, -1)
    v_first = None
    for i in range(depth):
        if i == 0:
            w_in_i, mu_i = w_in_first, mu_first
        else:
            w_in_i, mu_i = w_in_rest[i - 1], mu_rest[i - 1]
        win = jnp.pad(w_in_i, ((0, 0), (0, P_COLS - w_in_i.shape[1]))).astype(BF16)
        mu = jnp.pad(mu_i, (0, Q_COLS - mu_i.shape[0])).reshape(1, Q_COLS)
        wd = _pad_rows(decay_up[i], 0, LORA_BLK).astype(BF16)
        wa = _pad_rows(iclr_up[i], W_LORA, LORA_BLK).astype(BF16)
        wg = _pad_rows(gate_up[i], W_LORA + A_LORA, LORA_BLK).astype(BF16)
        vres = None
        if i > 0:
            wv = _pad_rows(vres_up[i - 1], W_LORA + A_LORA + G_LORA, LORA_BLK).astype(BF16)
            vres = (row(vres_v0[i - 1]), wv, v_first)
        r, lw, k, v, kk, bvec, g, c = _inproj(
            x, row(norm_pre_mix[i]), win, mu, row(decay_w0[i]), wd, row(iclr_a0[i]), wa, wg,
            row(k_k[i]), row(k_a[i]), seg, vres)
        if i == 0:
            v_first = v
        cw = jnp.pad(cconv_w[i], ((0, CONV_HALO - CONV_WIDTH), (0, 0)))
        wo = w_out[i].astype(BF16)
        x = _wkvmix(r, lw, k, v, kk, bvec, g, c, x, row(lnx_g[i]), row(lnx_b[i]), row(r_k[i]), cw,
                    row(cconv_b[i]), row(cln_g[i]), row(cln_b[i]), wo[:D_RWKV], wo[D_RWKV:],
                    row(norm_post_mix[i]), seg)
        fcw = jnp.pad(ffn_conv_w[i], ((0, SUBLANES - FFN_CONV_WIDTH), (0, 0)))
        x = _ffn(x, row(norm_pre_ffn[i]), ffn_up[i].astype(BF16), fcw, row(ffn_conv_b[i]),
                 ffn_down[i].astype(BF16), row(norm_post_ffn[i]))
    return x
```

```python
import functools
import math

import jax
import jax.numpy as jnp
import numpy as np
from jax.experimental import pallas as pl
from jax.experimental.pallas import tpu as pltpu

F32 = jnp.float32
BF16 = jnp.bfloat16

D_MODEL = 1024
D_RWKV = 512
D_CONV = 512
HEAD_DIM = 64
W_LORA, A_LORA, V_LORA, G_LORA = 32, 32, 32, 96
CONV_WIDTH = 31
FFN_CONV_WIDTH = 3
D_FF = 2816
GLU_COLS = 2 * D_CONV
RWKV_COLS = 3 * D_RWKV + W_LORA + A_LORA + G_LORA
RMS_EPS = 1e-6
LN_EPS = 1e-5
GN_EPS = 64e-5

LANES = 128
SUBLANES = 8
MXU_DIM = 256
LORA_OFF = 3 * D_RWKV
LORA_BLK = 2 * LANES
Q_COLS = LORA_OFF + LORA_BLK
P_COLS = GLU_COLS + Q_COLS
CHUNK = 64
GROUP = MXU_DIM
HEADS_PER_GROUP = GROUP // HEAD_DIM
N_GROUPS = D_RWKV // GROUP
CONV_HALO = 32
CONV_ROWS = 32
FF_BLK = 256
FF_ROWS = 64
FF_AHEAD = 3
VMEM_LIMIT = 56 * 1024 * 1024

TT_IN = 256
TT_WKV = 256
TT_FFN = 256


def _dot(a, b):
    return jax.lax.dot_general(a, b, (((1,), (0,)), ((), ())), preferred_element_type=F32)


def _dot_nt(a, b):
    return jax.lax.dot_general(a, b, (((1,), (1,)), ((), ())), preferred_element_type=F32)


def _bdot(a, w_bf16):
    return _dot(a.astype(BF16), w_bf16)


def _seg_sum(x, seg_ones):
    xb = x.astype(BF16)
    return jnp.concatenate([_dot(xb[:, i:i + MXU_DIM], seg_ones)
                            for i in range(0, x.shape[1], MXU_DIM)], axis=1)


def _rms(x, g):
    return x * jax.lax.rsqrt(jnp.mean(x * x, axis=-1, keepdims=True) + RMS_EPS) * g


def _shift_rows(x, carry, n):
    out = pltpu.roll(x, n, 0)
    head = out[:SUBLANES]
    row = jax.lax.broadcasted_iota(jnp.int32, head.shape, 0)
    for i in range(n):
        head = jnp.where(row == i, carry[SUBLANES - n + i:SUBLANES - n + i + 1, :], head)
    return jnp.concatenate([head, out[SUBLANES:]], axis=0)


def _inproj_kernel(has_vres, x_ref, gpre_ref, win_ref, mu_ref, w0_ref, wd_ref, a0_ref, wa_ref,
                   wg_ref, kk_ref, ka_ref, seg_ref, *rest):
    if has_vres:
        (v0_ref, wv_ref, vfirst_ref, r_out, lw_out, k_out, v_out, kk_out, b_out, g_out, c_out,
         carry_ref) = rest
    else:
        (r_out, lw_out, k_out, v_out, kk_out, b_out, g_out, c_out, carry_ref) = rest
    t = pl.program_id(1)

    @pl.when(t == 0)
    def _():
        carry_ref[...] = jnp.zeros_like(carry_ref)

    x = x_ref[0]
    tt = x.shape[0]
    h = _rms(x, gpre_ref[...]).astype(BF16)

    def lerp(cols):
        q = _dot(h, win_ref[:, pl.ds(GLU_COLS + cols.start, cols.stop - cols.start)])
        prev = _shift_rows(q, carry_ref[:, cols], 1)
        carry_ref[:, cols] = q[tt - SUBLANES:, :]
        return q + (prev - q) * mu_ref[:, cols]

    blk = lerp(slice(LORA_OFF, Q_COLS))
    u = _dot(h, win_ref[:, 0:GLU_COLS])
    wf = w0_ref[...] + _bdot(jnp.tanh(blk), wd_ref[...])
    lw_out[0] = -math.exp(-0.5) * jax.nn.sigmoid(wf)
    a = jax.nn.sigmoid(a0_ref[...] + _bdot(blk, wa_ref[...]))
    g_out[0] = _bdot(jax.nn.sigmoid(blk), wg_ref[...])
    if has_vres:
        vmix = jax.nn.sigmoid(v0_ref[...] + _bdot(blk, wv_ref[...]))
    q = lerp(slice(0, LORA_OFF))
    c_out[0] = u[:, :D_CONV] * jax.nn.sigmoid(u[:, D_CONV:])
    r = q[:, :D_RWKV]
    k = q[:, D_RWKV:2 * D_RWKV]
    v = q[:, 2 * D_RWKV:3 * D_RWKV]
    if has_vres:
        v = v + (vfirst_ref[0] - v) * vmix
    kk = k * kk_ref[...]
    ss = _seg_sum(kk * kk, seg_ref[...])
    kk = kk * jax.lax.rsqrt(jnp.maximum(ss, 1e-24))
    r_out[0] = r
    k_out[0] = k * (1.0 + (a - 1.0) * ka_ref[...])
    v_out[0] = v
    kk_out[0] = kk
    b_out[0] = kk * a


def _full(shape):
    return pl.BlockSpec(shape, lambda b, t: (0,) * len(shape))


def _tile(tt, width):
    return pl.BlockSpec((1, tt, width), lambda b, t: (b, t, 0))


def _params():
    return pltpu.CompilerParams(dimension_semantics=("arbitrary", "arbitrary"),
                                vmem_limit_bytes=VMEM_LIMIT)


def _inproj(x, gpre, win, mu, w0, wd, a0, wa, wg, k_k, k_a, seg, vres):
    B, T, _ = x.shape
    tt = TT_IN
    has_vres = vres is not None
    row = lambda w: _full((1, w))
    in_specs = [_tile(tt, D_MODEL), row(D_MODEL), _full((D_MODEL, P_COLS)), row(Q_COLS),
                row(D_RWKV), _full((LORA_BLK, D_RWKV)), row(D_RWKV), _full((LORA_BLK, D_RWKV)),
                _full((LORA_BLK, D_RWKV)), row(D_RWKV), row(D_RWKV), _full((MXU_DIM, MXU_DIM))]
    args = [x, gpre, win, mu, w0, wd, a0, wa, wg, k_k, k_a, seg]
    if has_vres:
        v0, wv, vfirst = vres
        in_specs += [row(D_RWKV), _full((LORA_BLK, D_RWKV)), _tile(tt, D_RWKV)]
        args += [v0, wv, vfirst]
    out = jax.ShapeDtypeStruct((B, T, D_RWKV), F32)
    return pl.pallas_call(
        functools.partial(_inproj_kernel, has_vres),
        grid=(B, T // tt),
        in_specs=in_specs,
        out_specs=[_tile(tt, D_RWKV)] * 8,
        out_shape=[out] * 8,
        scratch_shapes=[pltpu.VMEM((SUBLANES, Q_COLS), F32)],
        compiler_params=_params(),
        name="inproj",
    )(*args)


def _wkv_masks():
    i = np.arange(GROUP)
    same = (i[:, None] // CHUNK) == (i[None, :] // HEAD_DIM)
    rin, cin = i[:, None] % CHUNK, i[None, :] % CHUNK
    m = np.stack([same, same & (rin > cin), same & (rin >= cin), i[:, None] == i[None, :]])
    t = np.arange(CHUNK)
    return (jnp.asarray(m.astype(np.float32), dtype=BF16),
            jnp.asarray((t[:, None] >= t[None, :]).astype(np.float32), dtype=BF16))


def _wkvmix_kernel(r_ref, lw_ref, k_ref, v_ref, kk_ref, b_ref, g_ref, c_ref, x_ref, mask_ref,
                   tri_ref, lng_ref, lnb_ref, rk_ref, cw_ref, cb_ref, clg_ref, clb_ref, wo_ref,
                   gpost_ref, seg_ref, o_ref, z_ref, ext_ref, sh_ref, acc_ref, ak_ref, tinv_ref):
    t = pl.program_id(1)
    tt = c_ref.shape[1]

    @pl.when(t == 0)
    def _():
        z_ref[...] = jnp.zeros_like(z_ref)
        ext_ref[0:CONV_HALO, :] = jnp.zeros((CONV_HALO, D_CONV), F32)

    def bd(xl):
        return jnp.concatenate([xl.astype(BF16)] * HEADS_PER_GROUP, axis=0) * mask_ref[0]

    def fold(xb):
        out = xb[0:CHUNK]
        for hh in range(1, HEADS_PER_GROUP):
            out = out + xb[hh * CHUNK:(hh + 1) * CHUNK]
        return out

    def masked(x, which):
        return x.astype(BF16) * mask_ref[which]

    ext_ref[CONV_HALO:CONV_HALO + tt, :] = c_ref[0]
    span = tt + CONV_HALO - SUBLANES
    sh_ref[0] = ext_ref[...]
    for b in range(1, SUBLANES):
        sh_ref[b, 0:span, :] = ext_ref[pl.ds(b, span), :]
    ext_ref[0:CONV_HALO, :] = ext_ref[tt:tt + CONV_HALO, :]
    off = CONV_HALO - (CONV_WIDTH - 1)
    acc_ref[...] = (sh_ref[off % SUBLANES, pl.ds(off // SUBLANES * SUBLANES, tt), :] * cw_ref[0:1, :]
                    + cb_ref[...])

    n_chunks = tt // CHUNK
    insts = [(c, gi) for c in range(n_chunks) for gi in range(N_GROUPS)]
    eye = mask_ref[3]
    st = []
    tri = tri_ref[...]
    for c, gi in insts:
        rows, lanes = pl.ds(c * CHUNK, CHUNK), pl.ds(gi * GROUP, GROUP)
        lw = lw_ref[0, rows, lanes]
        p1 = lw.astype(BF16)
        r1 = lw - p1.astype(F32)
        p2 = r1.astype(BF16)
        p3 = (r1 - p2.astype(F32)).astype(BF16)
        st.append(dict(lw=lw, G=_dot(tri, p1) + _dot(tri, p2) + _dot(tri, p3)))
    for (c, gi), s in zip(insts, st):
        rows, lanes = pl.ds(c * CHUNK, CHUNK), pl.ds(gi * GROUP, GROUP)
        G, lw = s["G"], s["lw"]
        glast = G[CHUNK - 1:CHUNK, :]
        eG, eGp, enG, eGC = jnp.exp(G), jnp.exp(G - lw), jnp.exp(-G), jnp.exp(glast - G)
        r, k, v = r_ref[0, rows, lanes], k_ref[0, rows, lanes], v_ref[0, rows, lanes]
        a, b = -kk_ref[0, rows, lanes], b_ref[0, rows, lanes]
        rt = r * eG
        s.update(rt=rt, am=bd(a * eGp), rm=bd(rt), bm=bd(b * enG), km=bd(k * enG), vm=bd(v),
                 bht=(b * eGC).T.astype(BF16), kht=(k * eGC).T.astype(BF16),
                 vb=v.astype(BF16), pc=jnp.exp(glast))
    for i, s in enumerate(st):
        a_ab = masked(_dot_nt(s["am"], s["bm"]), 1)
        ak_ref[i] = a_ab
        tinv_ref[i] = a_ab + eye
    for s in st:
        s["a_ak"] = masked(_dot_nt(s["am"], s["km"]), 1)
    for s in st:
        s["a_rb"] = masked(_dot_nt(s["rm"], s["bm"]), 2)
    for s in st:
        last = _dot_nt(s["rm"], s["km"])
        s["a_rk"] = masked(last, 2)

    def zero_after(x):
        u = jax.lax.bitcast_convert_type(x[0:SUBLANES, 0:LANES], jnp.uint32)
        z = jax.lax.bitcast_convert_type((u >> 16) >> 16, F32)
        z = jnp.concatenate([z] * (D_CONV // LANES), axis=1)
        return jnp.concatenate([z] * (CONV_ROWS // SUBLANES), axis=0)

    n_sq = int(math.log2(CHUNK)) - 1
    tap_slices = [[int(j) for j in grp] for grp in np.array_split(np.arange(1, CONV_WIDTH), n_sq)]

    def conv_slice(after):
        zero = zero_after(after)
        taps = tap_slices.pop(0)
        for r0 in range(0, tt, CONV_ROWS):
            acc = acc_ref[r0:r0 + CONV_ROWS, :] + zero
            for j in taps:
                a, b = divmod(off + j, SUBLANES)
                acc = acc + sh_ref[b, pl.ds(a * SUBLANES + r0, CONV_ROWS), :] * cw_ref[j:j + 1, :]
            acc_ref[r0:r0 + CONV_ROWS, :] = acc

    for it in range(n_sq):
        conv_slice(last)
        for i in range(len(insts)):
            ak = ak_ref[i]
            ak_ref[i] = _dot(ak, ak).astype(BF16)
        for i in range(len(insts)):
            last = _dot(tinv_ref[i], ak_ref[i] + eye)
            tinv_ref[i] = last.astype(BF16)
    for i, s in enumerate(st):
        s["tinv"] = tinv_ref[i]
    assert not tap_slices
    for s in st:
        s["av"] = _dot(s["a_ak"], s["vm"]).astype(BF16)
    for s in st:
        s["u0"] = _dot(s["tinv"], s["av"])
        s["wt"] = _dot(s["tinv"], s["am"])

    same = mask_ref[0].astype(F32)
    for s in st:
        u0, wt = s["u0"], s["wt"]
        s["y0"] = fold(_dot(s["a_rk"], s["vm"]) + _dot(s["a_rb"], u0.astype(BF16)))
        s["rbar"] = (s["rt"] + fold(_dot(s["a_rb"], wt.astype(BF16)))).astype(BF16)
        m = _dot(s["bht"], fold(wt).astype(BF16)) * same + eye.astype(F32) * s["pc"]
        s["m"] = m.astype(BF16)
        s["q"] = (_dot(s["bht"], fold(u0).astype(BF16)) + _dot(s["kht"], s["vb"])) * same
    conv = acc_ref[...]
    mu = jnp.mean(conv, axis=-1, keepdims=True)
    cc = conv - mu
    cvar = jnp.mean(cc * cc, axis=-1, keepdims=True)
    cn = cc * jax.lax.rsqrt(cvar + LN_EPS) * clg_ref[...] + clb_ref[...]
    mix_c = _bdot(cn * jax.nn.sigmoid(cn), wo_ref[D_RWKV:, :])
    z = [z_ref[gi] for gi in range(N_GROUPS)]
    ys = []
    for (c, gi), s in zip(insts, st):
        z0 = z[gi].astype(BF16)
        ys.append(s["y0"] + _dot(s["rbar"], z0))
        z[gi] = _dot(s["m"], z0) + s["q"]
    for gi in range(N_GROUPS):
        z_ref[gi] = z[gi]
    y = jnp.concatenate([jnp.concatenate(ys[c * N_GROUPS:(c + 1) * N_GROUPS], axis=1)
                         for c in range(n_chunks)], axis=0)

    seg = seg_ref[...]
    inv_n = 1.0 / HEAD_DIM
    mean = _seg_sum(y, seg) * inv_n
    yc = y - mean
    var = _seg_sum(yc * yc, seg) * inv_n
    gn = yc * jax.lax.rsqrt(var + GN_EPS) * lng_ref[...] + lnb_ref[...]
    bonus = _seg_sum(r_ref[0] * k_ref[0] * rk_ref[...], seg) * v_ref[0]
    rw = (gn + bonus) * g_ref[0]
    mix = _bdot(rw, wo_ref[0:D_RWKV, :]) + mix_c
    o_ref[0] = x_ref[0] + _rms(mix, gpost_ref[...])


def _wkvmix(r, lw, k, v, kk, bvec, g, c, x, lng, lnb, rk, cw, cb, clg, clb, wo, gpost, seg):
    B, T, _ = x.shape
    tt = TT_WKV
    n_inst = tt // CHUNK * N_GROUPS
    masks, tri = _wkv_masks()
    row = lambda w: _full((1, w))
    return pl.pallas_call(
        _wkvmix_kernel,
        grid=(B, T // tt),
        in_specs=[_tile(tt, D_RWKV)] * 8 + [_tile(tt, D_MODEL), _full(masks.shape),
                                            _full(tri.shape), row(D_RWKV), row(D_RWKV),
                                            row(D_RWKV), _full((CONV_HALO, D_CONV)), row(D_CONV),
                                            row(D_CONV), row(D_CONV),
                                            _full((D_RWKV + D_CONV, D_MODEL)), row(D_MODEL),
                                            _full((MXU_DIM, MXU_DIM))],
        out_specs=_tile(tt, D_MODEL),
        out_shape=jax.ShapeDtypeStruct((B, T, D_MODEL), F32),
        scratch_shapes=[pltpu.VMEM((N_GROUPS, GROUP, GROUP), F32),
                        pltpu.VMEM((CONV_HALO + tt, D_CONV), F32),
                        pltpu.VMEM((SUBLANES, CONV_HALO + tt, D_CONV), F32),
                        pltpu.VMEM((tt, D_CONV), F32),
                        pltpu.VMEM((n_inst, GROUP, GROUP), BF16),
                        pltpu.VMEM((n_inst, GROUP, GROUP), BF16)],
        compiler_params=_params(),
        name="wkvmix",
    )(r, lw, k, v, kk, bvec, g, c, x, masks, tri, lng, lnb, rk, cw, cb, clg, clb, wo, gpost, seg)


def _ffn_kernel(x_ref, gpre_ref, up_ref, cw_ref, cb_ref, down_ref, gpost_ref, o_ref, carry_ref):
    t = pl.program_id(1)

    @pl.when(t == 0)
    def _():
        carry_ref[...] = jnp.zeros_like(carry_ref)

    x = x_ref[0]
    tt = x.shape[0]
    h = _rms(x, gpre_ref[...]).astype(BF16)
    n_blk = D_FF // FF_BLK

    def up_proj(cblk):
        return [_dot(h, up_ref[:, pl.ds(base + cblk * FF_BLK, FF_BLK)]) for base in (0, D_FF)]

    def conv(hid, carry, cols):
        h1 = _shift_rows(hid, carry, 1)
        h2 = _shift_rows(hid, carry, 2)
        return (h2 * cw_ref[0:1, cols] + h1 * cw_ref[1:2, cols] + hid * cw_ref[2:3, cols]
                + cb_ref[:, cols])

    def glu(hid, cblk):
        cg, cu = pl.ds(cblk * FF_BLK, FF_BLK), pl.ds(D_FF + cblk * FF_BLK, FF_BLK)
        hg, hu = hid
        carry_g, carry_u = carry_ref[:, cg], carry_ref[:, cu]
        carry_ref[:, cg] = hg[tt - SUBLANES:, :]
        carry_ref[:, cu] = hu[tt - SUBLANES:, :]
        acts = []
        for r0 in range(0, tt, FF_ROWS):
            if r0:
                carry_g, carry_u = hg[r0 - SUBLANES:r0], hu[r0 - SUBLANES:r0]
            gate = conv(hg[r0:r0 + FF_ROWS], carry_g, cg)
            upv = conv(hu[r0:r0 + FF_ROWS], carry_u, cu)
            acts.append(jax.nn.gelu(gate.astype(BF16), approximate=True) * upv.astype(BF16))
        return jnp.concatenate(acts, axis=0)

    acc = jnp.zeros((tt, D_MODEL), F32)
    hids = [up_proj(c) for c in range(min(FF_AHEAD, n_blk))]
    for cblk in range(n_blk):
        if cblk + FF_AHEAD < n_blk:
            hids.append(up_proj(cblk + FF_AHEAD))
        acc = acc + _dot(glu(hids[cblk], cblk), down_ref[pl.ds(cblk * FF_BLK, FF_BLK), :])
    o_ref[0] = x + _rms(acc, gpost_ref[...])


def _ffn(x, gpre, up, cw, cb, down, gpost):
    B, T, _ = x.shape
    tt = TT_FFN
    row = lambda w: _full((1, w))
    return pl.pallas_call(
        _ffn_kernel,
        grid=(B, T // tt),
        in_specs=[_tile(tt, D_MODEL), row(D_MODEL), _full((D_MODEL, 2 * D_FF)),
                  _full((SUBLANES, 2 * D_FF)), row(2 * D_FF), _full((D_FF, D_MODEL)),
                  row(D_MODEL)],
        out_specs=_tile(tt, D_MODEL),
        out_shape=jax.ShapeDtypeStruct((B, T, D_MODEL), F32),
        scratch_shapes=[pltpu.VMEM((SUBLANES, 2 * D_FF), F32)],
        compiler_params=_params(),
        name="convffn",
    )(x, gpre, up, cw, cb, down, gpost)


def _pad_rows(w, off, rows):
    return jnp.zeros((rows, w.shape[1]), w.dtype).at[off:off + w.shape[0]].set(w)


def _seg_ones():
    idx = np.arange(MXU_DIM) // HEAD_DIM
    return jnp.asarray((idx[:, None] == idx[None, :]).astype(np.float32), dtype=BF16)


def kernel(x, w_in_first, mu_first, w_in_rest, mu_rest, vres_v0, vres_up, decay_w0, decay_up, iclr_a0, iclr_up, gate_up, k_k, k_a, r_k, lnx_g, lnx_b, cconv_w, cconv_b, cln_g, cln_b, w_out, norm_pre_mix, norm_post_mix, norm_pre_ffn, norm_post_ffn, ffn_up, ffn_conv_w, ffn_conv_b, ffn_down):
    depth = decay_w0.shape[0]
    seg = _seg_ones()
    row = lambda a: a.reshape(1, -1)
    v_first = None
    for i in range(depth):
        if i == 0:
            w_in_i, mu_i = w_in_first, mu_first
        else:
            w_in_i, mu_i = w_in_rest[i - 1], mu_rest[i - 1]
        win = jnp.pad(w_in_i, ((0, 0), (0, P_COLS - w_in_i.shape[1]))).astype(BF16)
        mu = jnp.pad(mu_i, (0, Q_COLS - mu_i.shape[0])).reshape(1, Q_COLS)
        wd = _pad_rows(decay_up[i], 0, LORA_BLK).astype(BF16)
        wa = _pad_rows(iclr_up[i], W_LORA, LORA_BLK).astype(BF16)
        wg = _pad_rows(gate_up[i], W_LORA + A_LORA, LORA_BLK).astype(BF16)
        vres = None
        if i > 0:
            wv = _pad_rows(vres_up[i - 1], W_LORA + A_LORA + G_LORA, LORA_BLK).astype(BF16)
            vres = (row(vres_v0[i - 1]), wv, v_first)
        r, lw, k, v, kk, bvec, g, c = _inproj(
            x, row(norm_pre_mix[i]), win, mu, row(decay_w0[i]), wd, row(iclr_a0[i]), wa, wg,
            row(k_k[i]), row(k_a[i]), seg, vres)
        if i == 0:
            v_first = v
        cw = jnp.pad(cconv_w[i], ((0, CONV_HALO - CONV_WIDTH), (0, 0)))
        x = _wkvmix(r, lw, k, v, kk, bvec, g, c, x, row(lnx_g[i]), row(lnx_b[i]), row(r_k[i]), cw,
                    row(cconv_b[i]), row(cln_g[i]), row(cln_b[i]), w_out[i].astype(BF16),
                    row(norm_post_mix[i]), seg)
        fcw = jnp.pad(ffn_conv_w[i], ((0, SUBLANES - FFN_CONV_WIDTH), (0, 0)))
        x = _ffn(x, row(norm_pre_ffn[i]), ffn_up[i].astype(BF16), fcw, row(ffn_conv_b[i]),
                 ffn_down[i].astype(BF16), row(norm_post_ffn[i]))
    return x
```
